```python
import math
import jax, jax.numpy as jnp
from jax import lax
import numpy as np

D_MODEL = 1024
BATCH = 8
SEQ = 2048
DEPTH = 4
DEC_BATCH = 128
DEC_SEQ = 1
PAST_LEN = 16384
PAGE_SIZE = 128

N_MIXERS = 2
N_CONV_LAYERS = (DEPTH + 1) // 2
N_SSD_LAYERS = DEPTH // 2
D_FF = 2816
PLE_DIM = 256
CM_KERNEL = 31
EXPAND = 2
D_INNER = EXPAND * D_MODEL
HEAD_DIM = 64
N_SSD_HEADS = D_INNER // HEAD_DIM
N_GROUPS = 8
HEADS_PER_GROUP = N_SSD_HEADS // N_GROUPS
D_STATE = 128
SSD_CONV_K = 4
SSD_CHUNK = 128
CONV_DIM = D_INNER + 2 * N_GROUPS * D_STATE
SSD_IN_DIM = D_INNER + CONV_DIM + N_SSD_HEADS
EPS = 1e-6

kernel_name = "macaron_conformer_ssd_hybrid_step"


def rmsnorm(x, g):
    xf = x.astype(jnp.float32)
    y = xf * lax.rsqrt(jnp.mean(xf * xf, axis=-1, keepdims=True) + EPS) * g.astype(jnp.float32)
    return y.astype(x.dtype)


def layernorm(x, g, b):
    xf = x.astype(jnp.float32)
    mu = jnp.mean(xf, axis=-1, keepdims=True)
    var = jnp.mean(jnp.square(xf - mu), axis=-1, keepdims=True)
    y = (xf - mu) * lax.rsqrt(var + EPS) * g.astype(jnp.float32) + b.astype(jnp.float32)
    return y.astype(x.dtype)


def swiglu(x, wg, wu, wd):
    return (jax.nn.silu(x @ wg) * (x @ wu)) @ wd


def causal_dwconv(u, buf, w, b):
    k = w.shape[0]
    full = jnp.concatenate([buf.astype(u.dtype), u], axis=1)
    y = lax.conv_general_dilated(full, w[:, None, :].astype(u.dtype), window_strides=(1,), padding='VALID',
                                 dimension_numbers=('NWC', 'WIO', 'NWC'), feature_group_count=u.shape[-1])
    return y + b, full[:, full.shape[1] - (k - 1):]


def ssd_chunked(x, dt, A, Bm, Cm, h0, chunk):
    b, l, H, P = x.shape
    G, N = Bm.shape[2], Bm.shape[3]
    R = H // G
    c = l // chunk
    xdt = (x * dt[..., None]).reshape(b, c, chunk, G, R, P)
    a = (dt * A).reshape(b, c, chunk, G, R)
    Bc = Bm.reshape(b, c, chunk, G, N)
    Cc = Cm.reshape(b, c, chunk, G, N)
    a_cum = jnp.cumsum(a, axis=2)
    seg = a_cum[:, :, :, None] - a_cum[:, :, None, :]
    causal = jnp.tril(jnp.ones((chunk, chunk), dtype=bool))[:, :, None, None]
    Lmat = jnp.exp(jnp.where(causal, seg, -jnp.inf))
    cb = jnp.einsum('bcqgn,bckgn->bcqkg', Cc, Bc)
    y_diag = jnp.einsum('bcqkg,bcqkgr,bckgrp->bcqgrp', cb, Lmat, xdt)
    decay = jnp.exp(a_cum[:, :, -1:] - a_cum)
    states = jnp.einsum('bckgn,bckgr,bckgrp->bcgrpn', Bc, decay, xdt)
    chunk_decay = jnp.exp(a_cum[:, :, -1])

    def step(h, inp):
        s, d = inp
        return d[..., None, None] * h + s, h

    h_final, h_prev = lax.scan(step, h0, (jnp.swapaxes(states, 0, 1), jnp.swapaxes(chunk_decay, 0, 1)))
    h_prev = jnp.swapaxes(h_prev, 0, 1)
    y_off = jnp.einsum('bcqgn,bcgrpn,bcqgr->bcqgrp', Cc, h_prev, jnp.exp(a_cum))
    return (y_diag + y_off).reshape(b, l, H, P), h_final


def conv_mixer(h, buf, w_in, b_in, dw, dw_b, ln_g, ln_b, w_out, b_out):
    u = h @ w_in + b_in
    a, g = jnp.split(u, 2, axis=-1)
    u = a * jax.nn.sigmoid(g)
    v, new_buf = causal_dwconv(u, buf, dw, dw_b)
    v = jax.nn.silu(layernorm(v, ln_g, ln_b))
    return v @ w_out + b_out, new_buf


def ssd_mixer(h, buf, h0, w_in, conv_w, conv_b, dt_bias, A_log, D_skip, norm_g, w_out, chunk):
    b, l, _ = h.shape
    proj = h @ w_in
    z = proj[..., :D_INNER]
    xbc = proj[..., D_INNER:D_INNER + CONV_DIM]
    dt_raw = proj[..., D_INNER + CONV_DIM:]
    xbc, new_buf = causal_dwconv(xbc, buf, conv_w, conv_b)
    xbc = jax.nn.silu(xbc).astype(jnp.float32)
    GN = N_GROUPS * D_STATE
    xs = xbc[..., :D_INNER].reshape(b, l, N_SSD_HEADS, HEAD_DIM)
    Bm = xbc[..., D_INNER:D_INNER + GN].reshape(b, l, N_GROUPS, D_STATE)
    Cm = xbc[..., D_INNER + GN:].reshape(b, l, N_GROUPS, D_STATE)
    dt = jax.nn.softplus(dt_raw.astype(jnp.float32) + dt_bias.astype(jnp.float32))
    A = -jnp.exp(A_log.astype(jnp.float32))
    h0f = h0.astype(jnp.float32).reshape(b, N_GROUPS, HEADS_PER_GROUP, HEAD_DIM, D_STATE)
    y, h_fin = ssd_chunked(xs, dt, A, Bm, Cm, h0f, chunk)
    y = y + D_skip.astype(jnp.float32)[:, None] * xs
    y = y.reshape(b, l, D_INNER) * jax.nn.silu(z.astype(jnp.float32))
    yg = y.reshape(b, l, N_GROUPS, D_INNER // N_GROUPS)
    yg = yg * lax.rsqrt(jnp.mean(yg * yg, axis=-1, keepdims=True) + EPS)
    y = (yg.reshape(b, l, D_INNER) * norm_g.astype(jnp.float32)).astype(h.dtype)
    return y @ w_out, new_buf, h_fin.reshape(b, N_SSD_HEADS, HEAD_DIM, D_STATE).astype(h.dtype)


def trunk(x, p, conv_state, ssd_conv_state, ssd_state,
          norm_ffn1, w_ffn1_gate, w_ffn1_up, w_ffn1_down, norm_mix,
          norm_ffn2, w_ffn2_gate, w_ffn2_up, w_ffn2_down, norm_ple, w_ple_gate, w_ple_proj,
          cm_w_in, cm_b_in, cm_dw, cm_dw_b, cm_ln_g, cm_ln_b, cm_w_out, cm_b_out,
          ssd_w_in, ssd_conv_w, ssd_conv_b, ssd_dt_bias, ssd_A_log, ssd_D, ssd_norm, ssd_w_out, final_norm):
    L = x.shape[1]
    chunk = SSD_CHUNK if L % SSD_CHUNK == 0 else L
    conv_out, xbc_out, ssm_out = [], [], []
    for i in range(DEPTH):
        x = x + 0.5 * swiglu(rmsnorm(x, norm_ffn1[i]), w_ffn1_gate[i], w_ffn1_up[i], w_ffn1_down[i])
        hn = rmsnorm(x, norm_mix[i])
        j = i // N_MIXERS
        if i % N_MIXERS == 0:
            out, nb = conv_mixer(hn, conv_state[j], cm_w_in[j], cm_b_in[j], cm_dw[j], cm_dw_b[j],
                                 cm_ln_g[j], cm_ln_b[j], cm_w_out[j], cm_b_out[j])
            conv_out.append(nb)
        else:
            out, nb, nh = ssd_mixer(hn, ssd_conv_state[j], ssd_state[j], ssd_w_in[j], ssd_conv_w[j], ssd_conv_b[j],
                                    ssd_dt_bias[j], ssd_A_log[j], ssd_D[j], ssd_norm[j], ssd_w_out[j], chunk)
            xbc_out.append(nb)
            ssm_out.append(nh)
        x = x + out
        x = x + 0.5 * swiglu(rmsnorm(x, norm_ffn2[i]), w_ffn2_gate[i], w_ffn2_up[i], w_ffn2_down[i])
        gate = jax.nn.sigmoid(rmsnorm(x, norm_ple[i]) @ w_ple_gate[i])
        x = x + gate * (p[i].astype(x.dtype) @ w_ple_proj[i])
    return rmsnorm(x, final_norm), jnp.stack(conv_out), jnp.stack(xbc_out), jnp.stack(ssm_out)


def setup_inputs(seed: int = 0) -> dict:
    key = jax.random.key(seed)
    ks = iter(jax.random.split(key, 64))
    f32 = jnp.float32

    def nrm(shape, scale):
        return jax.random.normal(next(ks), shape, f32) * scale

    def gain(shape):
        return 1.0 + nrm(shape, 0.02)

    NC, NS = N_CONV_LAYERS, N_SSD_LAYERS
    dt0 = jnp.exp(jax.random.uniform(next(ks), (NS, N_SSD_HEADS), f32, math.log(1e-3), math.log(1e-1)))
    return {
        "x_prompt": nrm((BATCH, SEQ, D_MODEL), 1.0),
        "x_sample": nrm((DEC_BATCH, DEC_SEQ, D_MODEL), 1.0),
        "state_conv": nrm((NC, DEC_BATCH, CM_KERNEL - 1, D_MODEL), 0.5),
        "state_ssd_conv": nrm((NS, DEC_BATCH, SSD_CONV_K - 1, CONV_DIM), 1.0),
        "state_ssd": nrm((NS, DEC_BATCH, N_SSD_HEADS, HEAD_DIM, D_STATE), 0.1),
        "p_prompt": nrm((DEPTH, BATCH, SEQ, PLE_DIM), 1.0),
        "p_sample": nrm((DEPTH, DEC_BATCH, DEC_SEQ, PLE_DIM), 1.0),
        "norm_ffn1": gain((DEPTH, D_MODEL)),
        "w_ffn1_gate": nrm((DEPTH, D_MODEL, D_FF), D_MODEL ** -0.5),
        "w_ffn1_up": nrm((DEPTH, D_MODEL, D_FF), D_MODEL ** -0.5),
        "w_ffn1_down": nrm((DEPTH, D_FF, D_MODEL), D_FF ** -0.5),
        "norm_mix": gain((DEPTH, D_MODEL)),
        "norm_ffn2": gain((DEPTH, D_MODEL)),
        "w_ffn2_gate": nrm((DEPTH, D_MODEL, D_FF), D_MODEL ** -0.5),
        "w_ffn2_up": nrm((DEPTH, D_MODEL, D_FF), D_MODEL ** -0.5),
        "w_ffn2_down": nrm((DEPTH, D_FF, D_MODEL), D_FF ** -0.5),
        "norm_ple": gain((DEPTH, D_MODEL)),
        "w_ple_gate": nrm((DEPTH, D_MODEL, D_MODEL), D_MODEL ** -0.5),
        "w_ple_proj": nrm((DEPTH, PLE_DIM, D_MODEL), PLE_DIM ** -0.5),
        "cm_w_in": nrm((NC, D_MODEL, 2 * D_MODEL), D_MODEL ** -0.5),
        "cm_b_in": nrm((NC, 2 * D_MODEL), 0.02),
        "cm_dw": nrm((NC, CM_KERNEL, D_MODEL), CM_KERNEL ** -0.5),
        "cm_dw_b": nrm((NC, D_MODEL), 0.02),
        "cm_ln_g": gain((NC, D_MODEL)),
        "cm_ln_b": nrm((NC, D_MODEL), 0.02),
        "cm_w_out": nrm((NC, D_MODEL, D_MODEL), D_MODEL ** -0.5),
        "cm_b_out": nrm((NC, D_MODEL), 0.02),
        "ssd_w_in": nrm((NS, D_MODEL, SSD_IN_DIM), D_MODEL ** -0.5),
        "ssd_conv_w": nrm((NS, SSD_CONV_K, CONV_DIM), SSD_CONV_K ** -0.5),
        "ssd_conv_b": nrm((NS, CONV_DIM), 0.02),
        "ssd_dt_bias": dt0 + jnp.log(-jnp.expm1(-dt0)),
        "ssd_A_log": jnp.log(jax.random.uniform(next(ks), (NS, N_SSD_HEADS), f32, 1.0, 16.0)),
        "ssd_D": gain((NS, N_SSD_HEADS)),
        "ssd_norm": gain((NS, D_INNER)),
        "ssd_w_out": nrm((NS, D_INNER, D_MODEL), D_INNER ** -0.5),
        "final_norm": gain((D_MODEL,)),
    }


def reference(x_prompt, x_sample, state_conv, state_ssd_conv, state_ssd, p_prompt, p_sample,
              norm_ffn1, w_ffn1_gate, w_ffn1_up, w_ffn1_down, norm_mix,
              norm_ffn2, w_ffn2_gate, w_ffn2_up, w_ffn2_down, norm_ple, w_ple_gate, w_ple_proj,
              cm_w_in, cm_b_in, cm_dw, cm_dw_b, cm_ln_g, cm_ln_b, cm_w_out, cm_b_out,
              ssd_w_in, ssd_conv_w, ssd_conv_b, ssd_dt_bias, ssd_A_log, ssd_D, ssd_norm, ssd_w_out, final_norm):
    weights = (norm_ffn1, w_ffn1_gate, w_ffn1_up, w_ffn1_down, norm_mix,
               norm_ffn2, w_ffn2_gate, w_ffn2_up, w_ffn2_down, norm_ple, w_ple_gate, w_ple_proj,
               cm_w_in, cm_b_in, cm_dw, cm_dw_b, cm_ln_g, cm_ln_b, cm_w_out, cm_b_out,
               ssd_w_in, ssd_conv_w, ssd_conv_b, ssd_dt_bias, ssd_A_log, ssd_D, ssd_norm, ssd_w_out, final_norm)
    dt = x_prompt.dtype
    zc = jnp.zeros((N_CONV_LAYERS, BATCH, CM_KERNEL - 1, D_MODEL), dt)
    zx = jnp.zeros((N_SSD_LAYERS, BATCH, SSD_CONV_K - 1, CONV_DIM), dt)
    zs = jnp.zeros((N_SSD_LAYERS, BATCH, N_SSD_HEADS, HEAD_DIM, D_STATE), dt)
    y_prompt, conv_p, xbc_p, ssm_p = trunk(x_prompt, p_prompt, zc, zx, zs, *weights)
    y_sample, conv_s, xbc_s, ssm_s = trunk(x_sample, p_sample, state_conv, state_ssd_conv, state_ssd, *weights)
    return (y_prompt, y_sample, conv_p, xbc_p, ssm_p, conv_s, xbc_s, ssm_s)
```

```python
import functools

import jax
import jax.numpy as jnp
from jax import lax
from jax.experimental import pallas as pl
from jax.experimental.pallas import tpu as pltpu

F32 = jnp.float32
BF16 = jnp.bfloat16
EPS = 1e-6

V7X_LANES = 128
V7X_SUBLANES = 8
V7X_VMEM_LIMIT_BYTES = 56 * 1024 * 1024

SSD_CHUNK = 128
TOKEN_TILE = 512
CONV_TILE = 512
SAMPLE_CONV_TILE = 32
SAMPLE_STATE_TILE = 8


def _cparams(*sem):
    return pltpu.CompilerParams(dimension_semantics=sem, vmem_limit_bytes=V7X_VMEM_LIMIT_BYTES)


def _const_spec(shape):
    nd = len(shape)
    return pl.BlockSpec(shape, lambda *_: (0,) * nd, pipeline_mode=pl.Buffered(1))


def _dot(a, b):
    return jnp.dot(a, b, preferred_element_type=F32)


def _rmsnorm(x, g):
    return x * lax.rsqrt(jnp.mean(x * x, axis=-1, keepdims=True) + EPS) * g


def _layernorm(x, g, b):
    mu = jnp.mean(x, axis=-1, keepdims=True)
    xc = x - mu
    var = jnp.mean(xc * xc, axis=-1, keepdims=True)
    return xc * lax.rsqrt(var + EPS) * g + b


def _sigmoid(x):
    return 1.0 / (1.0 + jnp.exp(-x))


def _silu(x):
    return x * _sigmoid(x)


def _softplus(x):
    return jnp.maximum(x, 0.0) + jnp.log1p(jnp.exp(-jnp.abs(x)))


def _swiglu(x, g, wg_ref, wu_ref, wd_ref):
    h = _rmsnorm(x, g).astype(BF16)
    gate = _dot(h, wg_ref[...])
    up = _dot(h, wu_ref[...])
    act = (_silu(gate) * up).astype(BF16)
    return _dot(act, wd_ref[...])


def _ffn_kernel(x_ref, g_ref, wg_ref, wu_ref, wd_ref, o_ref):
    x = x_ref[...]
    o_ref[...] = x + 0.5 * _swiglu(x, g_ref[...], wg_ref, wu_ref, wd_ref)


def _ffn_ple_kernel(x_ref, p_ref, g_ref, wg_ref, wu_ref, wd_ref, gp_ref, wpg_ref, wpp_ref, fn_ref, o_ref,
                    *, final_norm):
    x = x_ref[...]
    x = x + 0.5 * _swiglu(x, g_ref[...], wg_ref, wu_ref, wd_ref)
    h = _rmsnorm(x, gp_ref[...]).astype(BF16)
    gate = _sigmoid(_dot(h, wpg_ref[...]))
    x = x + gate * _dot(p_ref[...].astype(BF16), wpp_ref[...])
    if final_norm:
        x = _rmsnorm(x, fn_ref[...])
    o_ref[...] = x


def _token_tile(m):
    return TOKEN_TILE if m % TOKEN_TILE == 0 else m


def _ffn(x, g, wg, wu, wd):
    m, d = x.shape
    tm = _token_tile(m)
    row = pl.BlockSpec((tm, d), lambda i: (i, 0))
    return pl.pallas_call(
        _ffn_kernel,
        out_shape=jax.ShapeDtypeStruct((m, d), F32),
        grid=(m // tm,),
        in_specs=[row, _const_spec(g.shape), _const_spec(wg.shape), _const_spec(wu.shape), _const_spec(wd.shape)],
        out_specs=row,
        compiler_params=_cparams("parallel"),
        name="ffn",
    )(x, g, wg, wu, wd)


def _ffn_ple(x, p, g, wg, wu, wd, gp, wpg, wpp, fn, final_norm):
    m, d = x.shape
    tm = _token_tile(m)
    row = pl.BlockSpec((tm, d), lambda i: (i, 0))
    prow = pl.BlockSpec((tm, p.shape[1]), lambda i: (i, 0))
    consts = (g, wg, wu, wd, gp, wpg, wpp, fn)
    return pl.pallas_call(
        functools.partial(_ffn_ple_kernel, final_norm=final_norm),
        out_shape=jax.ShapeDtypeStruct((m, d), F32),
        grid=(m // tm,),
        in_specs=[row, prow] + [_const_spec(c.shape) for c in consts],
        out_specs=row,
        compiler_params=_cparams("parallel"),
        name="ffn_ple",
    )(x, p, *consts)


def _glu_in(x, g, win_ref, bin_ref):
    d = x.shape[-1]
    u = _dot(_rmsnorm(x, g).astype(BF16), win_ref[...]) + bin_ref[...]
    return u[:, :d] * _sigmoid(u[:, d:])


def _conv_tail(x, v, lng, lnb, wout_ref, bout_ref):
    v = _silu(_layernorm(v, lng, lnb)).astype(BF16)
    return x + _dot(v, wout_ref[...]) + bout_ref[...]


def _conv_prompt_kernel(x_ref, g_ref, win_ref, bin_ref, dw_ref, dwb_ref, lng_ref, lnb_ref, wout_ref, bout_ref,
                        o_ref, nbuf_ref, full_ref, v_ref, *, taps, row_block):
    t = pl.program_id(1)
    tl = x_ref.shape[0]
    pad = full_ref.shape[0] - tl
    hist = taps - 1

    @pl.when(t == 0)
    def _():
        full_ref[0:pad, :] = jnp.zeros((pad, full_ref.shape[1]), F32)

    x = x_ref[...]
    full_ref[pad:pad + tl, :] = _glu_in(x, g_ref[...], win_ref, bin_ref)

    bias = dwb_ref[...]
    for r0 in range(0, tl, row_block):
        acc = jnp.broadcast_to(bias, (row_block, bias.shape[1]))
        for k in range(taps):
            acc = acc + dw_ref[k:k + 1, :] * full_ref[r0 + pad - hist + k:r0 + pad - hist + k + row_block, :]
        v_ref[r0:r0 + row_block, :] = acc
    o_ref[...] = _conv_tail(x, v_ref[...], lng_ref[...], lnb_ref[...], wout_ref, bout_ref)

    @pl.when(t == pl.num_programs(1) - 1)
    def _():
        nbuf_ref[...] = full_ref[pad + tl - hist:pad + tl, :]

    full_ref[0:pad, :] = full_ref[tl:tl + pad, :]


def _conv_prompt(x, g, win, b_in, dw, dwb, lng, lnb, wout, bout):
    b, l, d = x.shape
    taps = dw.shape[0]
    tl = CONV_TILE if l % CONV_TILE == 0 else l
    pad = -(-(taps - 1) // V7X_SUBLANES) * V7X_SUBLANES
    row = pl.BlockSpec((None, tl, d), lambda i, j: (i, j, 0))
    consts = (g, win, b_in, dw, dwb, lng, lnb, wout, bout)
    return pl.pallas_call(
        functools.partial(_conv_prompt_kernel, taps=taps, row_block=min(tl, 32)),
        out_shape=(jax.ShapeDtypeStruct((b, l, d), F32), jax.ShapeDtypeStruct((b, taps - 1, d), F32)),
        grid=(b, l // tl),
        in_specs=[row] + [_const_spec(c.shape) for c in consts],
        out_specs=(row, pl.BlockSpec((None, taps - 1, d), lambda i, j: (i, 0, 0))),
        scratch_shapes=[pltpu.VMEM((tl + pad, d), F32), pltpu.VMEM((tl, d), F32)],
        compiler_params=_cparams("parallel", "arbitrary"),
        name="conv_prompt",
    )(x, *consts)


def _conv_sample_kernel(x_ref, buf_ref, g_ref, win_ref, bin_ref, dw_ref, dwb_ref, lng_ref, lnb_ref, wout_ref,
                        bout_ref, o_ref, nbuf_ref, *, taps):
    x = x_ref[...]
    d = x.shape[-1]
    hist = taps - 1
    glu = _glu_in(x, g_ref[...], win_ref, bin_ref)
    acc = dwb_ref[...] + dw_ref[hist:taps, :] * glu
    for k in range(hist):
        acc = acc + dw_ref[k:k + 1, :] * buf_ref[:, k * d:(k + 1) * d]
    nbuf_ref[:, 0:(hist - 1) * d] = buf_ref[:, d:hist * d]
    nbuf_ref[:, (hist - 1) * d:hist * d] = glu
    o_ref[...] = _conv_tail(x, acc, lng_ref[...], lnb_ref[...], wout_ref, bout_ref)


def _conv_sample(x, buf, g, win, b_in, dw, dwb, lng, lnb, wout, bout):
    m, d = x.shape
    taps = dw.shape[0]
    tb = SAMPLE_CONV_TILE if m % SAMPLE_CONV_TILE == 0 else m
    row = pl.BlockSpec((tb, d), lambda i: (i, 0))
    brow = pl.BlockSpec((tb, buf.shape[1]), lambda i: (i, 0))
    consts = (g, win, b_in, dw, dwb, lng, lnb, wout, bout)
    return pl.pallas_call(
        functools.partial(_conv_sample_kernel, taps=taps),
        out_shape=(jax.ShapeDtypeStruct((m, d), F32), jax.ShapeDtypeStruct(buf.shape, F32)),
        grid=(m // tb,),
        in_specs=[row, brow] + [_const_spec(c.shape) for c in consts],
        out_specs=(row, brow),
        compiler_params=_cparams("parallel"),
        name="conv_sample",
    )(x, buf, *consts)


def _expand_heads(v, e_ref):
    hi = v.astype(BF16)
    r1 = v - hi.astype(F32)
    mid = r1.astype(BF16)
    lo = (r1 - mid.astype(F32)).astype(BF16)
    e = e_ref[...]
    return _dot(hi, e) + _dot(mid, e) + _dot(lo, e)


def _group_rmsnorm_gate(y, z, ng, n_groups):
    y = y * _silu(z)
    gw = y.shape[-1] // n_groups
    outs = []
    for g in range(n_groups):
        s = y[:, g * gw:(g + 1) * gw]
        outs.append(s * lax.rsqrt(jnp.mean(s * s, axis=-1, keepdims=True) + EPS))
    return jnp.concatenate(outs, axis=-1) * ng


def _ssd_prompt_kernel(x_ref, g_ref, wz_ref, wxbc_ref, wdt_ref, cw_ref, cb_ref, dtb_ref, alog_ref, dskip_ref,
                       ng_ref, wout_ref, e_ref, o_ref, nbuf_ref, hfin_ref, full_ref, ht_ref, y_ref,
                       *, n_groups, n_heads, head_dim, d_state):
    t = pl.program_id(1)
    q = x_ref.shape[0]
    d_inner = n_heads * head_dim
    gn = n_groups * d_state
    hpg = n_heads // n_groups
    gw = hpg * head_dim
    ck = cw_ref.shape[0]
    pad = full_ref.shape[0] - q

    @pl.when(t == 0)
    def _():
        full_ref[0:pad, :] = jnp.zeros((pad, full_ref.shape[1]), F32)
        ht_ref[...] = jnp.zeros(ht_ref.shape, F32)

    x = x_ref[...]
    hn = _rmsnorm(x, g_ref[...]).astype(BF16)
    z = _dot(hn, wz_ref[...])
    full_ref[pad:pad + q, :] = _dot(hn, wxbc_ref[...])
    dt = _softplus(_dot(hn, wdt_ref[...]) + dtb_ref[...])
    a = dt * (-jnp.exp(alog_ref[...]))

    xbc = jnp.broadcast_to(cb_ref[...], (q, full_ref.shape[1]))
    for k in range(ck):
        lo = pad - (ck - 1) + k
        xbc = xbc + cw_ref[k:k + 1, :] * full_ref[lo:lo + q, :]
    xbc = _silu(xbc)
    xs = xbc[:, :d_inner]
    bm = xbc[:, d_inner:d_inner + gn]
    cm = xbc[:, d_inner + gn:]

    rows = lax.broadcasted_iota(jnp.int32, a.shape, 0)
    acum = a
    sh = 1
    while sh < q:
        acum = acum + jnp.where(rows >= sh, pltpu.roll(acum, sh, 0), 0.0)
        sh *= 2
    acum_t = acum.T
    a_last = acum[q - 1:q, :]
    ex = _expand_heads(
        jnp.concatenate([dt, dt * jnp.exp(a_last - acum), jnp.exp(acum),
                         jnp.broadcast_to(jnp.exp(a_last), (V7X_SUBLANES, a.shape[1]))], axis=0), e_ref)
    dt_x = ex[0:q]
    decdt_x = ex[q:2 * q]
    expa_x = ex[2 * q:3 * q]
    cd_x = ex[3 * q:3 * q + 1]

    xdt = (xs * dt_x).astype(BF16)
    wdec = (xs * decdt_x).astype(BF16)
    causal = lax.broadcasted_iota(jnp.int32, (q, q), 0) >= lax.broadcasted_iota(jnp.int32, (q, q), 1)

    for g in range(n_groups):
        bg = bm[:, g * d_state:(g + 1) * d_state].astype(BF16)
        cg = cm[:, g * d_state:(g + 1) * d_state].astype(BF16)
        cbm = lax.dot_general(cg, bg, (((1,), (1,)), ((), ())), preferred_element_type=F32)
        gs = slice(g * gw, (g + 1) * gw)
        h_prev = ht_ref[:, gs]
        y_off = _dot(cg, h_prev.astype(BF16)) * expa_x[:, gs]
        for r in range(hpg):
            h = g * hpg + r
            seg = acum[:, h:h + 1] - acum_t[h:h + 1, :]
            lmat = jnp.exp(jnp.where(causal, seg, -jnp.inf))
            hs = slice(h * head_dim, (h + 1) * head_dim)
            y_ref[:, hs] = _dot((cbm * lmat).astype(BF16), xdt[:, hs])
        y_ref[:, gs] = y_ref[:, gs] + y_off
        upd = lax.dot_general(bg, wdec[:, gs], (((0,), (0,)), ((), ())), preferred_element_type=F32)
        ht_ref[:, gs] = h_prev * cd_x[:, gs] + upd

    y = y_ref[...] + dskip_ref[...] * xs
    yn = _group_rmsnorm_gate(y, z, ng_ref[...], n_groups).astype(BF16)
    o_ref[...] = x + _dot(yn, wout_ref[...])

    @pl.when(t == pl.num_programs(1) - 1)
    def _():
        nbuf_ref[...] = full_ref[pad + q - (ck - 1):pad + q, :]
        hfin_ref[...] = ht_ref[...].T

    full_ref[0:pad, :] = full_ref[q:q + pad, :]


def _ssd_prompt(x, g, wz, wxbc, wdt, cw, cb, dtb, alog, dskip, ng, wout, e, dims):
    b, l, d = x.shape
    n_groups, n_heads, head_dim, d_state = dims
    d_inner = n_heads * head_dim
    cd = wxbc.shape[1]
    ck = cw.shape[0]
    q = SSD_CHUNK if l % SSD_CHUNK == 0 else l
    row = pl.BlockSpec((None, q, d), lambda i, j: (i, j, 0))
    consts = (g, wz, wxbc, wdt, cw, cb, dtb, alog, dskip, ng, wout, e)
    return pl.pallas_call(
        functools.partial(_ssd_prompt_kernel, n_groups=n_groups, n_heads=n_heads, head_dim=head_dim,
                          d_state=d_state),
        out_shape=(jax.ShapeDtypeStruct((b, l, d), F32),
                   jax.ShapeDtypeStruct((b, ck - 1, cd), F32),
                   jax.ShapeDtypeStruct((b, d_inner, d_state), F32)),
        grid=(b, l // q),
        in_specs=[row] + [_const_spec(c.shape) for c in consts],
        out_specs=(row,
                   pl.BlockSpec((None, ck - 1, cd), lambda i, j: (i, 0, 0)),
                   pl.BlockSpec((None, d_inner, d_state), lambda i, j: (i, 0, 0))),
        scratch_shapes=[pltpu.VMEM((q + V7X_SUBLANES, cd), F32),
                        pltpu.VMEM((d_state, d_inner), F32),
                        pltpu.VMEM((q, d_inner), F32)],
        compiler_params=_cparams("parallel", "arbitrary"),
        name="ssd_prompt",
    )(x, *consts)


def _ssd_sample_in_kernel(x_ref, buf_ref, g_ref, wz_ref, wxbc_ref, wdt_ref, cw_ref, cb_ref, dtb_ref, alog_ref,
                          e_ref, z_ref, xs_ref, bm_ref, cmt_ref, xdtt_ref, expa_ref, nbuf_ref,
                          *, d_inner, gn):
    x = x_ref[...]
    hn = _rmsnorm(x, g_ref[...]).astype(BF16)
    z_ref[...] = _dot(hn, wz_ref[...])
    new = _dot(hn, wxbc_ref[...])
    cd = new.shape[1]
    ck = cw_ref.shape[0]
    xbc = cb_ref[...] + cw_ref[ck - 1:ck, :] * new
    for k in range(ck - 1):
        xbc = xbc + cw_ref[k:k + 1, :] * buf_ref[:, k * cd:(k + 1) * cd]
    nbuf_ref[:, 0:(ck - 2) * cd] = buf_ref[:, cd:(ck - 1) * cd]
    nbuf_ref[:, (ck - 2) * cd:(ck - 1) * cd] = new
    xbc = _silu(xbc)
    xs = xbc[:, :d_inner]
    dt = _softplus(_dot(hn, wdt_ref[...]) + dtb_ref[...])
    xs_ref[...] = xs
    bm_ref[...] = xbc[:, d_inner:d_inner + gn]
    cmt_ref[...] = xbc[:, d_inner + gn:].T
    xdtt_ref[...] = (xs * _expand_heads(dt, e_ref)).T
    expa_ref[...] = jnp.exp(dt * (-jnp.exp(alog_ref[...])))


def _ssd_sample_state_kernel(h0_ref, bm_ref, cmt_ref, xdtt_ref, expa_ref, hout_ref, yt_ref,
                             *, n_groups, n_heads, head_dim, d_state):
    t = pl.program_id(0)
    tb = h0_ref.shape[0]
    ntok = bm_ref.shape[0]
    hpg = n_heads // n_groups
    gw = hpg * head_dim

    @pl.when(t == 0)
    def _():
        yt_ref[...] = jnp.zeros(yt_ref.shape, F32)

    tok_rows = lax.broadcasted_iota(jnp.int32, (ntok, d_state), 0)
    tok_cols = lax.broadcasted_iota(jnp.int32, (d_state, ntok), 1)

    def body(i, carry):
        tok = t * tb + i
        ea = expa_ref[pl.ds(tok, 1), :]
        for g in range(n_groups):
            rs = slice(g * gw, (g + 1) * gw)
            ns = slice(g * d_state, (g + 1) * d_state)
            rb = jnp.where(tok_rows == tok, bm_ref[:, ns], 0.0).astype(BF16)
            upd = _dot(xdtt_ref[rs, :].astype(BF16), rb)
            parts = []
            for r in range(hpg):
                h = g * hpg + r
                hr = slice(h * head_dim, (h + 1) * head_dim)
                parts.append(h0_ref[i, hr, :] * ea[:, h:h + 1])
            hnew = jnp.concatenate(parts, axis=0) + upd
            hout_ref[i, rs, :] = hnew
            rc = jnp.where(tok_cols == tok, cmt_ref[ns, :], 0.0).astype(BF16)
            yt_ref[rs, :] = yt_ref[rs, :] + _dot(hnew.astype(BF16), rc)
        return carry

    lax.fori_loop(0, tb, body, 0)


def _ssd_sample_out_kernel(x_ref, yt_ref, xs_ref, z_ref, dskip_ref, ng_ref, wout_ref, o_ref, *, n_groups):
    y = yt_ref[...].T + dskip_ref[...] * xs_ref[...]
    yn = _group_rmsnorm_gate(y, z_ref[...], ng_ref[...], n_groups).astype(BF16)
    o_ref[...] = x_ref[...] + _dot(yn, wout_ref[...])


def _ssd_sample(x, buf, h0, g, wz, wxbc, wdt, cw, cb, dtb, alog, dskip, ng, wout, e, dims):
    m, d = x.shape
    n_groups, n_heads, head_dim, d_state = dims
    d_inner = n_heads * head_dim
    gn = n_groups * d_state
    consts = (g, wz, wxbc, wdt, cw, cb, dtb, alog, e)
    full = lambda a: pl.BlockSpec(a.shape, lambda *_: (0,) * a.ndim)
    outs = (jax.ShapeDtypeStruct((m, d_inner), F32),
            jax.ShapeDtypeStruct((m, d_inner), F32),
            jax.ShapeDtypeStruct((m, gn), F32),
            jax.ShapeDtypeStruct((gn, m), F32),
            jax.ShapeDtypeStruct((d_inner, m), F32),
            jax.ShapeDtypeStruct((m, V7X_LANES), F32),
            jax.ShapeDtypeStruct(buf.shape, F32))
    z, xs, bm, cmt, xdtt, expa, nbuf = pl.pallas_call(
        functools.partial(_ssd_sample_in_kernel, d_inner=d_inner, gn=gn),
        out_shape=outs,
        grid=(1,),
        in_specs=[full(x), full(buf)] + [_const_spec(c.shape) for c in consts],
        out_specs=tuple(pl.BlockSpec(o.shape, lambda *_, n=len(o.shape): (0,) * n) for o in outs),
        compiler_params=_cparams("arbitrary"),
        name="ssd_sample_in",
    )(x, buf, *consts)

    tb = SAMPLE_STATE_TILE if m % SAMPLE_STATE_TILE == 0 else m
    st = pl.BlockSpec((tb, d_inner, d_state), lambda i: (i, 0, 0))
    hout, yt = pl.pallas_call(
        functools.partial(_ssd_sample_state_kernel, n_groups=n_groups, n_heads=n_heads, head_dim=head_dim,
                          d_state=d_state),
        out_shape=(jax.ShapeDtypeStruct(h0.shape, F32), jax.ShapeDtypeStruct((d_inner, m), F32)),
        grid=(m // tb,),
        in_specs=[st, _const_spec(bm.shape), _const_spec(cmt.shape), _const_spec(xdtt.shape),
                  _const_spec(expa.shape)],
        out_specs=(st, pl.BlockSpec((d_inner, m), lambda i: (0, 0))),
        compiler_params=_cparams("arbitrary"),
        name="ssd_sample_state",
    )(h0, bm, cmt, xdtt, expa)

    out = pl.pallas_call(
        functools.partial(_ssd_sample_out_kernel, n_groups=n_groups),
        out_shape=jax.ShapeDtypeStruct((m, d), F32),
        grid=(1,),
        in_specs=[full(x), full(yt), full(xs), full(z), _const_spec(dskip.shape), _const_spec(ng.shape),
                  _const_spec(wout.shape)],
        out_specs=full(x),
        compiler_params=_cparams("arbitrary"),
        name="ssd_sample_out",
    )(x, yt, xs, z, dskip, ng, wout)
    return out, nbuf, hout


def _row(v):
    return v.reshape(1, -1).astype(F32)


def _pad_lanes(v):
    return jnp.pad(v, [(0, 0)] * (v.ndim - 1) + [(0, V7X_LANES - v.shape[-1])])


def kernel(x_prompt, x_sample, state_conv, state_ssd_conv, state_ssd, p_prompt, p_sample, norm_ffn1, w_ffn1_gate, w_ffn1_up, w_ffn1_down, norm_mix, norm_ffn2, w_ffn2_gate, w_ffn2_up, w_ffn2_down, norm_ple, w_ple_gate, w_ple_proj, cm_w_in, cm_b_in, cm_dw, cm_dw_b, cm_ln_g, cm_ln_b, cm_w_out, cm_b_out, ssd_w_in, ssd_conv_w, ssd_conv_b, ssd_dt_bias, ssd_A_log, ssd_D, ssd_norm, ssd_w_out, final_norm):
    depth = norm_ffn1.shape[0]
    bp, lp, d = x_prompt.shape
    bs = x_sample.shape[0]
    n_heads = ssd_dt_bias.shape[1]
    head_dim, d_state = state_ssd.shape[3], state_ssd.shape[4]
    d_inner = n_heads * head_dim
    conv_dim = ssd_conv_w.shape[2]
    n_groups = (conv_dim - d_inner) // (2 * d_state)
    dims = (n_groups, n_heads, head_dim, d_state)
    assert n_heads <= V7X_LANES and lp % SSD_CHUNK == 0 and x_sample.shape[1] == 1

    bf = lambda w: w.astype(BF16)
    w1g, w1u, w1d = bf(w_ffn1_gate), bf(w_ffn1_up), bf(w_ffn1_down)
    w2g, w2u, w2d = bf(w_ffn2_gate), bf(w_ffn2_up), bf(w_ffn2_down)
    wpg, wpp = bf(w_ple_gate), bf(w_ple_proj)
    cwin, cwout = bf(cm_w_in), bf(cm_w_out)
    swz = bf(ssd_w_in[:, :, :d_inner])
    swxbc = bf(ssd_w_in[:, :, d_inner:d_inner + conv_dim])
    swdt = bf(_pad_lanes(ssd_w_in[:, :, d_inner + conv_dim:]))
    swout = bf(ssd_w_out)
    dtb = _pad_lanes(ssd_dt_bias)
    alog = _pad_lanes(ssd_A_log)
    dskip = jnp.repeat(ssd_D, head_dim, axis=1)
    expand = (lax.broadcasted_iota(jnp.int32, (V7X_LANES, d_inner), 0)
              == lax.broadcasted_iota(jnp.int32, (V7X_LANES, d_inner), 1) // head_dim).astype(BF16)

    xp = x_prompt.reshape(bp * lp, d)
    xsm = x_sample.reshape(bs, d)
    pp = p_prompt.reshape(depth, bp * lp, -1)
    ps = p_sample.reshape(depth, bs, -1)
    conv_p, xbc_p, ssm_p, conv_s, xbc_s, ssm_s = [], [], [], [], [], []

    for i in range(depth):
        j = i // 2
        ffn1 = (_row(norm_ffn1[i]), w1g[i], w1u[i], w1d[i])
        xp = _ffn(xp, *ffn1)
        xsm = _ffn(xsm, *ffn1)
        if i % 2 == 0:
            cw = (_row(norm_mix[i]), cwin[j], _row(cm_b_in[j]), cm_dw[j], _row(cm_dw_b[j]), _row(cm_ln_g[j]),
                  _row(cm_ln_b[j]), cwout[j], _row(cm_b_out[j]))
            xp3, nb = _conv_prompt(xp.reshape(bp, lp, d), *cw)
            xp = xp3.reshape(bp * lp, d)
            conv_p.append(nb)
            xsm, nb = _conv_sample(xsm, state_conv[j].reshape(bs, -1), *cw)
            conv_s.append(nb.reshape(state_conv.shape[1:]))
        else:
            sw = (_row(norm_mix[i]), swz[j], swxbc[j], swdt[j], ssd_conv_w[j], _row(ssd_conv_b[j]), _row(dtb[j]),
                  _row(alog[j]), _row(dskip[j]), _row(ssd_norm[j]), swout[j], expand)
            xp3, nb, hf = _ssd_prompt(xp.reshape(bp, lp, d), *sw, dims)
            xp = xp3.reshape(bp * lp, d)
            xbc_p.append(nb)
            ssm_p.append(hf.reshape(bp, n_heads, head_dim, d_state))
            xsm, nb, hf = _ssd_sample(xsm, state_ssd_conv[j].reshape(bs, -1),
                                      state_ssd[j].reshape(bs, d_inner, d_state), *sw, dims)
            xbc_s.append(nb.reshape(state_ssd_conv.shape[1:]))
            ssm_s.append(hf.reshape(state_ssd.shape[1:]))
        tail = (_row(norm_ffn2[i]), w2g[i], w2u[i], w2d[i], _row(norm_ple[i]), wpg[i], wpp[i], _row(final_norm))
        last = i == depth - 1
        xp = _ffn_ple(xp, pp[i], *tail, final_norm=last)
        xsm = _ffn_ple(xsm, ps[i], *tail, final_norm=last)

    return (xp.reshape(bp, lp, d), xsm.reshape(bs, 1, d),
            jnp.stack(conv_p), jnp.stack(xbc_p), jnp.stack(ssm_p),
            jnp.stack(conv_s), jnp.stack(xbc_s), jnp.stack(ssm_s))
```

```python
import functools

import jax
import jax.numpy as jnp
from jax import lax
from jax.experimental import pallas as pl
from jax.experimental.pallas import tpu as pltpu

F32 = jnp.float32
BF16 = jnp.bfloat16
EPS = 1e-6

V7X_LANES = 128
V7X_SUBLANES = 8
V7X_VMEM_LIMIT_BYTES = 56 * 1024 * 1024

SSD_CHUNK = 128
SSD_TILE = 512
TOKEN_TILE = 512
CONV_TILE = 512
CONV_ROWS = 128
SAMPLE_CONV_TILE = 32
SAMPLE_STATE_TILE = 8


def _cparams(*sem):
    return pltpu.CompilerParams(dimension_semantics=sem, vmem_limit_bytes=V7X_VMEM_LIMIT_BYTES)


def _const_spec(arr):
    nd = arr.ndim
    return pl.BlockSpec(arr.shape, lambda *_: (0,) * nd, pipeline_mode=pl.Buffered(1))


def _layer_spec(arr, layer):
    nd = arr.ndim
    return pl.BlockSpec((None,) + arr.shape[1:], lambda *_: (layer,) + (0,) * (nd - 1),
                        pipeline_mode=pl.Buffered(1))


def _dot(a, b):
    return jnp.dot(a, b, preferred_element_type=F32)


def _rmsnorm(x, g):
    return x * lax.rsqrt(jnp.mean(x * x, axis=-1, keepdims=True) + EPS) * g


def _layernorm(x, g, b):
    mu = jnp.mean(x, axis=-1, keepdims=True)
    xc = x - mu
    var = jnp.mean(xc * xc, axis=-1, keepdims=True)
    return xc * lax.rsqrt(var + EPS) * g + b


def _sigmoid(x):
    return 1.0 / (1.0 + jnp.exp(-x))


def _silu(x):
    return x * _sigmoid(x)


def _softplus(x):
    return jnp.maximum(x, 0.0) + jnp.log(1.0 + jnp.exp(-jnp.abs(x)))


def _swiglu(x, g, wg_ref, wu_ref, wd_ref):
    h = _rmsnorm(x, g).astype(BF16)
    gate = _dot(h, wg_ref[...])
    up = _dot(h, wu_ref[...])
    act = (_silu(gate) * up).astype(BF16)
    return _dot(act, wd_ref[...])


def _ffn_kernel(x_ref, g_ref, wg_ref, wu_ref, wd_ref, o_ref):
    x = x_ref[...]
    o_ref[...] = x + 0.5 * _swiglu(x, g_ref[...], wg_ref, wu_ref, wd_ref)


def _ffn_ple_kernel(x_ref, p_ref, g_ref, wg_ref, wu_ref, wd_ref, gp_ref, wpg_ref, wpp_ref, fn_ref, o_ref,
                    *, final_norm):
    x = x_ref[...]
    x = x + 0.5 * _swiglu(x, g_ref[...], wg_ref, wu_ref, wd_ref)
    h = _rmsnorm(x, gp_ref[...]).astype(BF16)
    gate = _sigmoid(_dot(h, wpg_ref[...]))
    x = x + gate * _dot(p_ref[...].astype(BF16), wpp_ref[...])
    if final_norm:
        x = _rmsnorm(x, fn_ref[...])
    o_ref[...] = x


def _token_tile(m):
    return TOKEN_TILE if m % TOKEN_TILE == 0 else m


def _ffn(x, layer, g, wg, wu, wd):
    m, d = x.shape
    tm = _token_tile(m)
    row = pl.BlockSpec((tm, d), lambda i: (i, 0))
    consts = (g, wg, wu, wd)
    return pl.pallas_call(
        _ffn_kernel,
        out_shape=jax.ShapeDtypeStruct((m, d), F32),
        grid=(m // tm,),
        in_specs=[row] + [_layer_spec(c, layer) for c in consts],
        out_specs=row,
        compiler_params=_cparams("parallel"),
        name="ffn",
    )(x, *consts)


def _ffn_ple(x, p, layer, g, wg, wu, wd, gp, wpg, wpp, fn, final_norm):
    m, d = x.shape
    tm = _token_tile(m)
    row = pl.BlockSpec((tm, d), lambda i: (i, 0))
    prow = pl.BlockSpec((None, tm, p.shape[2]), lambda i: (layer, i, 0))
    consts = (g, wg, wu, wd, gp, wpg, wpp)
    return pl.pallas_call(
        functools.partial(_ffn_ple_kernel, final_norm=final_norm),
        out_shape=jax.ShapeDtypeStruct((m, d), F32),
        grid=(m // tm,),
        in_specs=[row, prow] + [_layer_spec(c, layer) for c in consts] + [_const_spec(fn)],
        out_specs=row,
        compiler_params=_cparams("parallel"),
        name="ffn_ple",
    )(x, p, *consts, fn)


def _glu_in(x, g, win_ref, bin_ref):
    d = x.shape[-1]
    u = _dot(_rmsnorm(x, g).astype(BF16), win_ref[...]) + bin_ref[...]
    return u[:, :d] * _sigmoid(u[:, d:])


def _conv_tail(x, v, lng, lnb, wout_ref, bout_ref):
    v = _silu(_layernorm(v, lng, lnb)).astype(BF16)
    return x + _dot(v, wout_ref[...]) + bout_ref[...]


def _conv_prompt_kernel(x_ref, g_ref, win_ref, bin_ref, dw_ref, dwb_ref, lng_ref, lnb_ref, wout_ref, bout_ref,
                        o_ref, nbuf_ref, slab_ref, v_ref, *, taps, row_block):
    t = pl.program_id(1)
    tl = x_ref.shape[0]
    nslab = slab_ref.shape[0]
    pad = slab_ref.shape[1] - tl
    hist = taps - 1

    @pl.when(t == 0)
    def _():
        slab_ref[:, 0:pad, :] = jnp.zeros((nslab, pad, V7X_LANES), F32)

    x = x_ref[...]
    glu = _glu_in(x, g_ref[...], win_ref, bin_ref)
    for j in range(nslab):
        slab_ref[j, pad:pad + tl, :] = glu[:, j * V7X_LANES:(j + 1) * V7X_LANES]

    for j in range(nslab):
        ls = slice(j * V7X_LANES, (j + 1) * V7X_LANES)
        for r0 in range(0, tl, row_block):
            acc = jnp.broadcast_to(dwb_ref[:, ls], (row_block, V7X_LANES))
            for k in range(taps):
                lo = r0 + pad - hist + k
                acc = acc + dw_ref[k:k + 1, ls] * slab_ref[j, lo:lo + row_block, :]
            v_ref[r0:r0 + row_block, ls] = acc
    o_ref[...] = _conv_tail(x, v_ref[...], lng_ref[...], lnb_ref[...], wout_ref, bout_ref)

    @pl.when(t == pl.num_programs(1) - 1)
    def _():
        for j in range(nslab):
            nbuf_ref[:, j * V7X_LANES:(j + 1) * V7X_LANES] = slab_ref[j, pad + tl - hist:pad + tl, :]

    slab_ref[:, 0:pad, :] = slab_ref[:, tl:tl + pad, :]


def _conv_prompt(x, norm_layer, layer, g, win, b_in, dw, dwb, lng, lnb, wout, bout):
    b, l, d = x.shape
    taps = dw.shape[1]
    tl = CONV_TILE if l % CONV_TILE == 0 else l
    pad = -(-(taps - 1) // V7X_SUBLANES) * V7X_SUBLANES
    row = pl.BlockSpec((None, tl, d), lambda i, j: (i, j, 0))
    consts = (win, b_in, dw, dwb, lng, lnb, wout, bout)
    return pl.pallas_call(
        functools.partial(_conv_prompt_kernel, taps=taps, row_block=min(tl, CONV_ROWS)),
        out_shape=(jax.ShapeDtypeStruct((b, l, d), F32), jax.ShapeDtypeStruct((b, taps - 1, d), F32)),
        grid=(b, l // tl),
        in_specs=[row, _layer_spec(g, norm_layer)] + [_layer_spec(c, layer) for c in consts],
        out_specs=(row, pl.BlockSpec((None, taps - 1, d), lambda i, j: (i, 0, 0))),
        scratch_shapes=[pltpu.VMEM((d // V7X_LANES, tl + pad, V7X_LANES), F32), pltpu.VMEM((tl, d), F32)],
        compiler_params=_cparams("parallel", "arbitrary"),
        name="conv_prompt",
    )(x, g, *consts)


def _conv_sample_kernel(x_ref, buf_ref, g_ref, win_ref, bin_ref, dw_ref, dwb_ref, lng_ref, lnb_ref, wout_ref,
                        bout_ref, o_ref, nbuf_ref, *, taps):
    x = x_ref[...]
    d = x.shape[-1]
    hist = taps - 1
    glu = _glu_in(x, g_ref[...], win_ref, bin_ref)
    acc = dwb_ref[...] + dw_ref[hist:taps, :] * glu
    for k in range(hist):
        acc = acc + dw_ref[k:k + 1, :] * buf_ref[:, k * d:(k + 1) * d]
    nbuf_ref[:, 0:(hist - 1) * d] = buf_ref[:, d:hist * d]
    nbuf_ref[:, (hist - 1) * d:hist * d] = glu
    o_ref[...] = _conv_tail(x, acc, lng_ref[...], lnb_ref[...], wout_ref, bout_ref)


def _conv_sample(x, buf, norm_layer, layer, g, win, b_in, dw, dwb, lng, lnb, wout, bout):
    m, d = x.shape
    taps = dw.shape[1]
    tb = SAMPLE_CONV_TILE if m % SAMPLE_CONV_TILE == 0 else m
    row = pl.BlockSpec((tb, d), lambda i: (i, 0))
    brow = pl.BlockSpec((tb, buf.shape[1]), lambda i: (i, 0))
    consts = (win, b_in, dw, dwb, lng, lnb, wout, bout)
    return pl.pallas_call(
        functools.partial(_conv_sample_kernel, taps=taps),
        out_shape=(jax.ShapeDtypeStruct((m, d), F32), jax.ShapeDtypeStruct(buf.shape, F32)),
        grid=(m // tb,),
        in_specs=[row, brow, _layer_spec(g, norm_layer)] + [_layer_spec(c, layer) for c in consts],
        out_specs=(row, brow),
        compiler_params=_cparams("parallel"),
        name="conv_sample",
    )(x, buf, g, *consts)


def _group_rmsnorm_gate(y, z, ng, n_groups):
    y = y * _silu(z)
    gw = y.shape[-1] // n_groups
    outs = []
    for g in range(n_groups):
        s = y[:, g * gw:(g + 1) * gw]
        outs.append(s * lax.rsqrt(jnp.mean(s * s, axis=-1, keepdims=True) + EPS))
    return jnp.concatenate(outs, axis=-1) * ng


def _prefix_sum_rows(a):
    n = a.shape[0]
    rows = lax.broadcasted_iota(jnp.int32, a.shape, 0)
    sh = 1
    while sh < n:
        a = a + jnp.where(rows >= sh, pltpu.roll(a, sh, 0), 0.0)
        sh *= 2
    return a


def _ssd_prompt_kernel(x_ref, g_ref, wz_ref, wxbc_ref, wdt_ref, cw_ref, cb_ref, dtb_ref, alog_ref, dskip_ref,
                       ng_ref, wout_ref, o_ref, nbuf_ref, hfin_ref, slab_ref, hn_ref, z_ref, dt_ref, y_ref, yn_ref,
                       ht_ref, *, n_groups, n_heads, head_dim, d_state, chunk, col_block, row_block):
    t = pl.program_id(1)
    tl = x_ref.shape[0]
    nslab = slab_ref.shape[0]
    pad = slab_ref.shape[1] - tl
    d_inner = n_heads * head_dim
    hpg = n_heads // n_groups
    gw = hpg * head_dim
    ck = cw_ref.shape[0]
    xslabs = d_inner // V7X_LANES
    assert d_state == V7X_LANES and gw % V7X_LANES == 0

    @pl.when(t == 0)
    def _():
        slab_ref[:, 0:pad, :] = jnp.zeros((nslab, pad, V7X_LANES), F32)
        ht_ref[...] = jnp.zeros(ht_ref.shape, F32)

    for r0 in range(0, tl, row_block):
        hn_ref[r0:r0 + row_block, :] = _rmsnorm(x_ref[r0:r0 + row_block, :], g_ref[...]).astype(BF16)
    hn = hn_ref[...]
    dt_ref[...] = _softplus(_dot(hn, wdt_ref[...]) + dtb_ref[...])
    for c0 in range(0, nslab * V7X_LANES, col_block):
        pre = _dot(hn, wxbc_ref[:, c0:c0 + col_block])
        for jj in range(col_block // V7X_LANES):
            j = c0 // V7X_LANES + jj
            ls = slice(j * V7X_LANES, (j + 1) * V7X_LANES)
            slab_ref[j, pad:pad + tl, :] = pre[:, jj * V7X_LANES:(jj + 1) * V7X_LANES]
            tail = slab_ref[j, tl:tl + pad, :]
            for r0 in reversed(range(0, tl, row_block)):
                acc = jnp.broadcast_to(cb_ref[:, ls], (row_block, V7X_LANES))
                for k in range(ck):
                    lo = r0 + pad - (ck - 1) + k
                    acc = acc + cw_ref[k:k + 1, ls] * slab_ref[j, lo:lo + row_block, :]
                slab_ref[j, pad + r0:pad + r0 + row_block, :] = _silu(acc)
            slab_ref[j, 0:pad, :] = tail
    z_ref[...] = _dot(hn, wz_ref[...])

    @pl.when(t == pl.num_programs(1) - 1)
    def _():
        for j in range(nslab):
            nbuf_ref[:, j * V7X_LANES:(j + 1) * V7X_LANES] = slab_ref[j, pad - (ck - 1):pad, :]

    neg_a = jnp.exp(alog_ref[...])
    hps = V7X_LANES // head_dim
    lane = lax.broadcasted_iota(jnp.int32, (1, V7X_LANES), 1)
    causal = lax.broadcasted_iota(jnp.int32, (chunk, chunk), 0) >= lax.broadcasted_iota(jnp.int32, (chunk, chunk), 1)

    def chunk_body(c, carry):
        r0 = pl.multiple_of(c * chunk, chunk)
        rows = pl.ds(pl.multiple_of(pad + c * chunk, V7X_SUBLANES), chunk)
        dt = dt_ref[pl.ds(r0, chunk), :]
        acum = _prefix_sum_rows(-dt * neg_a)
        a_last = acum[chunk - 1:chunk, :]
        acum_t = acum.T
        dt_t = dt.T
        decdt_t = (dt * jnp.exp(a_last - acum)).T
        cdec = jnp.exp(a_last)
        for g in range(n_groups):
            bg = slab_ref[xslabs + g, rows, :]
            cg = slab_ref[xslabs + n_groups + g, rows, :]
            cbm = lax.dot_general(cg.astype(BF16), bg.astype(BF16), (((1,), (1,)), ((), ())),
                                  preferred_element_type=F32)
            bg_t = bg.T
            for sl in range(g * gw // V7X_LANES, (g + 1) * gw // V7X_LANES):
                ls = slice(sl * V7X_LANES, (sl + 1) * V7X_LANES)
                xs_sl = slab_ref[sl, rows, :]
                hprev = ht_ref[:, ls]
                y_sl = upd_sl = cdec_sl = None
                for u in range(hps):
                    h = sl * hps + u
                    mine = (lane >= u * head_dim) & (lane < (u + 1) * head_dim)
                    xm = jnp.where(mine, xs_sl, 0.0).astype(BF16)
                    hm = jnp.where(mine, hprev, 0.0).astype(BF16)
                    colb = jnp.broadcast_to(acum[:, h:h + 1], (chunk, chunk))
                    lmat = jnp.exp(jnp.where(causal, colb - acum_t[h:h + 1, :], -jnp.inf))
                    lhs = jnp.concatenate([cbm * lmat * dt_t[h:h + 1, :], cg * jnp.exp(colb)], axis=1)
                    yh = _dot(lhs.astype(BF16), jnp.concatenate([xm, hm], axis=0))
                    uh = _dot((bg_t * decdt_t[h:h + 1, :]).astype(BF16), xm)
                    cd = jnp.where(mine, cdec[:, h:h + 1], 0.0)
                    y_sl = yh if u == 0 else y_sl + yh
                    upd_sl = uh if u == 0 else upd_sl + uh
                    cdec_sl = cd if u == 0 else cdec_sl + cd
                y_ref[pl.ds(r0, chunk), ls] = y_sl
                ht_ref[:, ls] = hprev * cdec_sl + upd_sl
        return carry

    lax.fori_loop(0, tl // chunk, chunk_body, 0)

    out = x_ref[...]
    for g in range(n_groups):
        gs = slice(g * gw, (g + 1) * gw)
        for r0 in range(0, tl, row_block):
            rs = slice(r0, r0 + row_block)
            xs = jnp.concatenate([slab_ref[j, pad + r0:pad + r0 + row_block, :]
                                  for j in range(g * gw // V7X_LANES, (g + 1) * gw // V7X_LANES)], axis=1)
            yg = (y_ref[rs, gs] + dskip_ref[:, gs] * xs) * _silu(z_ref[rs, gs])
            yg = yg * lax.rsqrt(jnp.mean(yg * yg, axis=-1, keepdims=True) + EPS) * ng_ref[:, gs]
            yn_ref[rs, gs] = yg.astype(BF16)
        out = out + _dot(yn_ref[:, gs], wout_ref[gs, :])
    o_ref[...] = out

    @pl.when(t == pl.num_programs(1) - 1)
    def _():
        hfin_ref[...] = ht_ref[...].T


def _ssd_prompt(x, norm_layer, layer, g, wz, wxbc, wdt, cw, cb, dtb, alog, dskip, ng, wout, dims):
    b, l, d = x.shape
    n_groups, n_heads, head_dim, d_state = dims
    d_inner = n_heads * head_dim
    cd = wxbc.shape[2]
    ck = cw.shape[1]
    tl = SSD_TILE if l % SSD_TILE == 0 else l
    q = SSD_CHUNK if tl % SSD_CHUNK == 0 else tl
    row = pl.BlockSpec((None, tl, d), lambda i, j: (i, j, 0))
    consts = (wz, wxbc, wdt, cw, cb, dtb, alog, dskip, ng, wout)
    return pl.pallas_call(
        functools.partial(_ssd_prompt_kernel, n_groups=n_groups, n_heads=n_heads, head_dim=head_dim,
                          d_state=d_state, chunk=q, col_block=min(cd, 2 * V7X_LANES), row_block=q),
        out_shape=(jax.ShapeDtypeStruct((b, l, d), F32),
                   jax.ShapeDtypeStruct((b, ck - 1, cd), F32),
                   jax.ShapeDtypeStruct((b, d_inner, d_state), F32)),
        grid=(b, l // tl),
        in_specs=[row, _layer_spec(g, norm_layer)] + [_layer_spec(c, layer) for c in consts],
        out_specs=(row,
                   pl.BlockSpec((None, ck - 1, cd), lambda i, j: (i, 0, 0)),
                   pl.BlockSpec((None, d_inner, d_state), lambda i, j: (i, 0, 0))),
        scratch_shapes=[pltpu.VMEM((cd // V7X_LANES, tl + V7X_SUBLANES, V7X_LANES), F32),
                        pltpu.VMEM((tl, d), BF16),
                        pltpu.VMEM((tl, d_inner), F32),
                        pltpu.VMEM((tl, V7X_LANES), F32),
                        pltpu.VMEM((tl, d_inner), F32),
                        pltpu.VMEM((tl, d_inner), BF16),
                        pltpu.VMEM((d_state, d_inner), F32)],
        compiler_params=_cparams("parallel", "arbitrary"),
        name="ssd_prompt",
    )(x, g, *consts)


def _expand_heads(v, e_ref):
    hi = v.astype(BF16)
    r1 = v - hi.astype(F32)
    mid = r1.astype(BF16)
    lo = (r1 - mid.astype(F32)).astype(BF16)
    e = e_ref[...]
    return _dot(hi, e) + _dot(mid, e) + _dot(lo, e)


def _ssd_sample_in_kernel(x_ref, buf_ref, g_ref, wz_ref, wxbc_ref, wdt_ref, cw_ref, cb_ref, dtb_ref, alog_ref,
                          e_ref, z_ref, xs_ref, bm_ref, cmt_ref, xdtt_ref, expa_ref, nbuf_ref,
                          *, d_inner, gn):
    x = x_ref[...]
    hn = _rmsnorm(x, g_ref[...]).astype(BF16)
    z_ref[...] = _dot(hn, wz_ref[...])
    new = _dot(hn, wxbc_ref[...])
    cd = new.shape[1]
    ck = cw_ref.shape[0]
    xbc = cb_ref[...] + cw_ref[ck - 1:ck, :] * new
    for k in range(ck - 1):
        xbc = xbc + cw_ref[k:k + 1, :] * buf_ref[:, k * cd:(k + 1) * cd]
    nbuf_ref[:, 0:(ck - 2) * cd] = buf_ref[:, cd:(ck - 1) * cd]
    nbuf_ref[:, (ck - 2) * cd:(ck - 1) * cd] = new
    xbc = _silu(xbc)
    xs = xbc[:, :d_inner]
    dt = _softplus(_dot(hn, wdt_ref[...]) + dtb_ref[...])
    xs_ref[...] = xs
    bm_ref[...] = xbc[:, d_inner:d_inner + gn]
    cmt_ref[...] = xbc[:, d_inner + gn:].T
    xdtt_ref[...] = (xs * _expand_heads(dt, e_ref)).T
    expa_ref[...] = jnp.exp(dt * (-jnp.exp(alog_ref[...])))


def _ssd_sample_state_kernel(h0_ref, bm_ref, cmt_ref, xdtt_ref, expa_ref, hout_ref, yt_ref,
                             *, n_groups, n_heads, head_dim, d_state):
    t = pl.program_id(0)
    tb = h0_ref.shape[0]
    ntok = bm_ref.shape[0]
    hpg = n_heads // n_groups
    gw = hpg * head_dim

    @pl.when(t == 0)
    def _():
        yt_ref[...] = jnp.zeros(yt_ref.shape, F32)

    tok_rows = lax.broadcasted_iota(jnp.int32, (ntok, d_state), 0)
    tok_cols = lax.broadcasted_iota(jnp.int32, (d_state, ntok), 1)

    def body(i, carry):
        tok = t * tb + i
        ea = expa_ref[pl.ds(tok, 1), :]
        for g in range(n_groups):
            rs = slice(g * gw, (g + 1) * gw)
            ns = slice(g * d_state, (g + 1) * d_state)
            rb = jnp.where(tok_rows == tok, bm_ref[:, ns], 0.0).astype(BF16)
            upd = _dot(xdtt_ref[rs, :].astype(BF16), rb)
            parts = []
            for r in range(hpg):
                h = g * hpg + r
                hr = slice(h * head_dim, (h + 1) * head_dim)
                parts.append(h0_ref[i, hr, :] * ea[:, h:h + 1])
            hnew = jnp.concatenate(parts, axis=0) + upd
            hout_ref[i, rs, :] = hnew
            rc = jnp.where(tok_cols == tok, cmt_ref[ns, :], 0.0).astype(BF16)
            yt_ref[rs, :] = yt_ref[rs, :] + _dot(hnew.astype(BF16), rc)
        return carry

    lax.fori_loop(0, tb, body, 0)


def _ssd_sample_out_kernel(x_ref, yt_ref, xs_ref, z_ref, dskip_ref, ng_ref, wout_ref, o_ref, *, n_groups):
    y = yt_ref[...].T + dskip_ref[...] * xs_ref[...]
    yn = _group_rmsnorm_gate(y, z_ref[...], ng_ref[...], n_groups).astype(BF16)
    o_ref[...] = x_ref[...] + _dot(yn, wout_ref[...])


def _ssd_sample(x, buf, h0_all, norm_layer, layer, g, wz, wxbc, wdt, cw, cb, dtb, alog, dskip, ng, wout, e, dims):
    m, d = x.shape
    n_groups, n_heads, head_dim, d_state = dims
    d_inner = n_heads * head_dim
    gn = n_groups * d_state
    consts = (wz, wxbc, wdt, cw, cb, dtb, alog)
    full = lambda a: pl.BlockSpec(a.shape, lambda *_: (0,) * a.ndim)
    outs = (jax.ShapeDtypeStruct((m, d_inner), F32),
            jax.ShapeDtypeStruct((m, d_inner), F32),
            jax.ShapeDtypeStruct((m, gn), F32),
            jax.ShapeDtypeStruct((gn, m), F32),
            jax.ShapeDtypeStruct((d_inner, m), F32),
            jax.ShapeDtypeStruct((m, V7X_LANES), F32),
            jax.ShapeDtypeStruct(buf.shape, F32))
    z, xs, bm, cmt, xdtt, expa, nbuf = pl.pallas_call(
        functools.partial(_ssd_sample_in_kernel, d_inner=d_inner, gn=gn),
        out_shape=outs,
        grid=(1,),
        in_specs=[full(x), full(buf), _layer_spec(g, norm_layer)] + [_layer_spec(c, layer) for c in consts]
        + [_const_spec(e)],
        out_specs=tuple(pl.BlockSpec(o.shape, lambda *_, n=len(o.shape): (0,) * n) for o in outs),
        compiler_params=_cparams("arbitrary"),
        name="ssd_sample_in",
    )(x, buf, g, *consts, e)

    tb = SAMPLE_STATE_TILE if m % SAMPLE_STATE_TILE == 0 else m
    st_in = pl.BlockSpec((None, tb, d_inner, d_state), lambda i: (layer, i, 0, 0))
    st_out = pl.BlockSpec((tb, d_inner, d_state), lambda i: (i, 0, 0))
    hout, yt = pl.pallas_call(
        functools.partial(_ssd_sample_state_kernel, n_groups=n_groups, n_heads=n_heads, head_dim=head_dim,
                          d_state=d_state),
        out_shape=(jax.ShapeDtypeStruct(h0_all.shape[1:], F32), jax.ShapeDtypeStruct((d_inner, m), F32)),
        grid=(m // tb,),
        in_specs=[st_in, _const_spec(bm), _const_spec(cmt), _const_spec(xdtt), _const_spec(expa)],
        out_specs=(st_out, pl.BlockSpec((d_inner, m), lambda i: (0, 0))),
        compiler_params=_cparams("arbitrary"),
        name="ssd_sample_state",
    )(h0_all, bm, cmt, xdtt, expa)

    out = pl.pallas_call(
        functools.partial(_ssd_sample_out_kernel, n_groups=n_groups),
        out_shape=jax.ShapeDtypeStruct((m, d), F32),
        grid=(1,),
        in_specs=[full(x), full(yt), full(xs), full(z), _layer_spec(dskip, layer), _layer_spec(ng, layer),
                  _layer_spec(wout, layer)],
        out_specs=full(x),
        compiler_params=_cparams("arbitrary"),
        name="ssd_sample_out",
    )(x, yt, xs, z, dskip, ng, wout)
    return out, nbuf, hout


def _rows(v):
    return v.reshape(v.shape[0], 1, v.shape[1]).astype(F32)


def _pad_lanes(v):
    return jnp.pad(v, [(0, 0)] * (v.ndim - 1) + [(0, V7X_LANES - v.shape[-1])])


def kernel(x_prompt, x_sample, state_conv, state_ssd_conv, state_ssd, p_prompt, p_sample, norm_ffn1, w_ffn1_gate, w_ffn1_up, w_ffn1_down, norm_mix, norm_ffn2, w_ffn2_gate, w_ffn2_up, w_ffn2_down, norm_ple, w_ple_gate, w_ple_proj, cm_w_in, cm_b_in, cm_dw, cm_dw_b, cm_ln_g, cm_ln_b, cm_w_out, cm_b_out, ssd_w_in, ssd_conv_w, ssd_conv_b, ssd_dt_bias, ssd_A_log, ssd_D, ssd_norm, ssd_w_out, final_norm):
    depth = norm_ffn1.shape[0]
    bp, lp, d = x_prompt.shape
    bs = x_sample.shape[0]
    n_heads = ssd_dt_bias.shape[1]
    head_dim, d_state = state_ssd.shape[3], state_ssd.shape[4]
    d_inner = n_heads * head_dim
    conv_dim = ssd_conv_w.shape[2]
    n_groups = (conv_dim - d_inner) // (2 * d_state)
    dims = (n_groups, n_heads, head_dim, d_state)
    assert n_heads <= V7X_LANES and lp % SSD_CHUNK == 0 and x_sample.shape[1] == 1

    bf = lambda w: w.astype(BF16)
    ffn1 = (_rows(norm_ffn1), bf(w_ffn1_gate), bf(w_ffn1_up), bf(w_ffn1_down))
    tail = (_rows(norm_ffn2), bf(w_ffn2_gate), bf(w_ffn2_up), bf(w_ffn2_down), _rows(norm_ple), bf(w_ple_gate),
            bf(w_ple_proj))
    fnorm = final_norm.reshape(1, d)
    nmix = _rows(norm_mix)
    cmw = (bf(cm_w_in), _rows(cm_b_in), cm_dw, _rows(cm_dw_b), _rows(cm_ln_g), _rows(cm_ln_b), bf(cm_w_out),
           _rows(cm_b_out))
    ssw = (bf(ssd_w_in[:, :, :d_inner]), bf(ssd_w_in[:, :, d_inner:d_inner + conv_dim]),
           bf(_pad_lanes(ssd_w_in[:, :, d_inner + conv_dim:])), ssd_conv_w, _rows(ssd_conv_b),
           _rows(_pad_lanes(ssd_dt_bias)), _rows(_pad_lanes(ssd_A_log)),
           _rows(jnp.repeat(ssd_D, head_dim, axis=1)), _rows(ssd_norm), bf(ssd_w_out))
    expand = (lax.broadcasted_iota(jnp.int32, (V7X_LANES, d_inner), 0)
              == lax.broadcasted_iota(jnp.int32, (V7X_LANES, d_inner), 1) // head_dim).astype(BF16)

    xp = x_prompt.reshape(bp * lp, d)
    xsm = x_sample.reshape(bs, d)
    pp = p_prompt.reshape(depth, bp * lp, -1)
    ps = p_sample.reshape(depth, bs, -1)
    h0_all = state_ssd.reshape(state_ssd.shape[0], bs, d_inner, d_state)
    conv_p, xbc_p, ssm_p, conv_s, xbc_s, ssm_s = [], [], [], [], [], []

    for i in range(depth):
        j = i // 2
        xp = _ffn(xp, i, *ffn1)
        xsm = _ffn(xsm, i, *ffn1)
        if i % 2 == 0:
            xp3, nb = _conv_prompt(xp.reshape(bp, lp, d), i, j, nmix, *cmw)
            xp = xp3.reshape(bp * lp, d)
            conv_p.append(nb)
            xsm, nb = _conv_sample(xsm, state_conv[j].reshape(bs, -1), i, j, nmix, *cmw)
            conv_s.append(nb.reshape(state_conv.shape[1:]))
        else:
            xp3, nb, hf = _ssd_prompt(xp.reshape(bp, lp, d), i, j, nmix, *ssw, dims)
            xp = xp3.reshape(bp * lp, d)
            xbc_p.append(nb)
            ssm_p.append(hf.reshape(bp, n_heads, head_dim, d_state))
            xsm, nb, hf = _ssd_sample(xsm, state_ssd_conv[j].reshape(bs, -1), h0_all, i, j, nmix, *ssw, expand,
                                      dims)
            xbc_s.append(nb.reshape(state_ssd_conv.shape[1:]))
            ssm_s.append(hf.reshape(state_ssd.shape[1:]))
        last = i == depth - 1
        xp = _ffn_ple(xp, pp, i, *tail, fnorm, final_norm=last)
        xsm = _ffn_ple(xsm, ps, i, *tail, fnorm, final_norm=last)

    return (xp.reshape(bp, lp, d), xsm.reshape(bs, 1, d),
            jnp.stack(conv_p), jnp.stack(xbc_p), jnp.stack(ssm_p),
            jnp.stack(conv_s), jnp.stack(xbc_s), jnp.stack(ssm_s))
```

```python
import functools

import jax
import jax.numpy as jnp
from jax import lax
from jax.experimental import pallas as pl
from jax.experimental.pallas import tpu as pltpu

F32 = jnp.float32
BF16 = jnp.bfloat16
EPS = 1e-6

V7X_LANES = 128
V7X_SUBLANES = 8
V7X_VMEM_LIMIT_BYTES = 56 * 1024 * 1024

SSD_CHUNK = 128
SSD_TILE = 512
TOKEN_TILE = 512
CONV_TILE = 512
CONV_ROWS = 128
SAMPLE_CONV_TILE = 32
SAMPLE_STATE_TILE = 8


def _cparams(*sem):
    return pltpu.CompilerParams(dimension_semantics=sem, vmem_limit_bytes=V7X_VMEM_LIMIT_BYTES)


def _const_spec(arr):
    nd = arr.ndim
    return pl.BlockSpec(arr.shape, lambda *_: (0,) * nd, pipeline_mode=pl.Buffered(1))


def _layer_spec(arr, layer):
    nd = arr.ndim
    return pl.BlockSpec((None,) + arr.shape[1:], lambda *_: (layer,) + (0,) * (nd - 1),
                        pipeline_mode=pl.Buffered(1))


def _dot(a, b):
    return jnp.dot(a, b, preferred_element_type=F32)


def _rmsnorm(x, g):
    return x * lax.rsqrt(jnp.mean(x * x, axis=-1, keepdims=True) + EPS) * g


def _layernorm(x, g, b):
    mu = jnp.mean(x, axis=-1, keepdims=True)
    xc = x - mu
    var = jnp.mean(xc * xc, axis=-1, keepdims=True)
    return xc * lax.rsqrt(var + EPS) * g + b


def _sigmoid(x):
    return 0.5 * jnp.tanh(0.5 * x) + 0.5


def _silu(x):
    h = 0.5 * x
    return h * jnp.tanh(h) + h


def _softplus(x):
    return jnp.maximum(x, 0.0) + jnp.log(1.0 + jnp.exp(-jnp.abs(x)))


def _swiglu(x, g, wg_ref, wu_ref, wd_ref):
    h = _rmsnorm(x, g).astype(BF16)
    gate = _dot(h, wg_ref[...])
    up = _dot(h, wu_ref[...])
    act = (_silu(gate) * up).astype(BF16)
    return _dot(act, wd_ref[...])


def _ffn_kernel(x_ref, g_ref, wg_ref, wu_ref, wd_ref, o_ref):
    x = x_ref[...]
    o_ref[...] = x + 0.5 * _swiglu(x, g_ref[...], wg_ref, wu_ref, wd_ref)


def _ffn_ple_kernel(x_ref, p_ref, g_ref, wg_ref, wu_ref, wd_ref, gp_ref, wpg_ref, wpp_ref, fn_ref, o_ref,
                    *, final_norm):
    x = x_ref[...]
    x = x + 0.5 * _swiglu(x, g_ref[...], wg_ref, wu_ref, wd_ref)
    h = _rmsnorm(x, gp_ref[...]).astype(BF16)
    gate = _sigmoid(_dot(h, wpg_ref[...]))
    x = x + gate * _dot(p_ref[...].astype(BF16), wpp_ref[...])
    if final_norm:
        x = _rmsnorm(x, fn_ref[...])
    o_ref[...] = x


def _token_tile(m):
    return TOKEN_TILE if m % TOKEN_TILE == 0 else m


def _ffn(x, layer, g, wg, wu, wd):
    m, d = x.shape
    tm = _token_tile(m)
    row = pl.BlockSpec((tm, d), lambda i: (i, 0))
    consts = (g, wg, wu, wd)
    return pl.pallas_call(
        _ffn_kernel,
        out_shape=jax.ShapeDtypeStruct((m, d), F32),
        grid=(m // tm,),
        in_specs=[row] + [_layer_spec(c, layer) for c in consts],
        out_specs=row,
        compiler_params=_cparams("parallel"),
        name="ffn",
    )(x, *consts)


def _ffn_ple(x, p, layer, g, wg, wu, wd, gp, wpg, wpp, fn, final_norm):
    m, d = x.shape
    tm = _token_tile(m)
    row = pl.BlockSpec((tm, d), lambda i: (i, 0))
    prow = pl.BlockSpec((None, tm, p.shape[2]), lambda i: (layer, i, 0))
    consts = (g, wg, wu, wd, gp, wpg, wpp)
    return pl.pallas_call(
        functools.partial(_ffn_ple_kernel, final_norm=final_norm),
        out_shape=jax.ShapeDtypeStruct((m, d), F32),
        grid=(m // tm,),
        in_specs=[row, prow] + [_layer_spec(c, layer) for c in consts] + [_const_spec(fn)],
        out_specs=row,
        compiler_params=_cparams("parallel"),
        name="ffn_ple",
    )(x, p, *consts, fn)


def _glu_in(x, g, win_ref, bin_ref):
    d = x.shape[-1]
    u = _dot(_rmsnorm(x, g).astype(BF16), win_ref[...]) + bin_ref[...]
    return u[:, :d] * _sigmoid(u[:, d:])


def _conv_tail(x, v, lng, lnb, wout_ref, bout_ref):
    v = _silu(_layernorm(v, lng, lnb)).astype(BF16)
    return x + _dot(v, wout_ref[...]) + bout_ref[...]


def _conv_prompt_kernel(x_ref, g_ref, win_ref, bin_ref, dw_ref, dwb_ref, lng_ref, lnb_ref, wout_ref, bout_ref,
                        o_ref, nbuf_ref, slab_ref, v_ref, *, taps, row_block):
    t = pl.program_id(1)
    tl = x_ref.shape[0]
    nslab = slab_ref.shape[0]
    pad = slab_ref.shape[1] - tl
    hist = taps - 1

    @pl.when(t == 0)
    def _():
        slab_ref[:, 0:pad, :] = jnp.zeros((nslab, pad, V7X_LANES), F32)

    x = x_ref[...]
    glu = _glu_in(x, g_ref[...], win_ref, bin_ref)
    for j in range(nslab):
        slab_ref[j, pad:pad + tl, :] = glu[:, j * V7X_LANES:(j + 1) * V7X_LANES]

    for j in range(nslab):
        ls = slice(j * V7X_LANES, (j + 1) * V7X_LANES)
        for r0 in range(0, tl, row_block):
            acc = jnp.broadcast_to(dwb_ref[:, ls], (row_block, V7X_LANES))
            for k in range(taps):
                lo = r0 + pad - hist + k
                acc = acc + dw_ref[k:k + 1, ls] * slab_ref[j, lo:lo + row_block, :]
            v_ref[r0:r0 + row_block, ls] = acc
    o_ref[...] = _conv_tail(x, v_ref[...], lng_ref[...], lnb_ref[...], wout_ref, bout_ref)

    @pl.when(t == pl.num_programs(1) - 1)
    def _():
        for j in range(nslab):
            nbuf_ref[:, j * V7X_LANES:(j + 1) * V7X_LANES] = slab_ref[j, pad + tl - hist:pad + tl, :]

    slab_ref[:, 0:pad, :] = slab_ref[:, tl:tl + pad, :]


def _conv_prompt(x, norm_layer, layer, g, win, b_in, dw, dwb, lng, lnb, wout, bout):
    b, l, d = x.shape
    taps = dw.shape[1]
    tl = CONV_TILE if l % CONV_TILE == 0 else l
    pad = -(-(taps - 1) // V7X_SUBLANES) * V7X_SUBLANES
    row = pl.BlockSpec((None, tl, d), lambda i, j: (i, j, 0))
    consts = (win, b_in, dw, dwb, lng, lnb, wout, bout)
    return pl.pallas_call(
        functools.partial(_conv_prompt_kernel, taps=taps, row_block=min(tl, CONV_ROWS)),
        out_shape=(jax.ShapeDtypeStruct((b, l, d), F32), jax.ShapeDtypeStruct((b, taps - 1, d), F32)),
        grid=(b, l // tl),
        in_specs=[row, _layer_spec(g, norm_layer)] + [_layer_spec(c, layer) for c in consts],
        out_specs=(row, pl.BlockSpec((None, taps - 1, d), lambda i, j: (i, 0, 0))),
        scratch_shapes=[pltpu.VMEM((d // V7X_LANES, tl + pad, V7X_LANES), F32), pltpu.VMEM((tl, d), F32)],
        compiler_params=_cparams("parallel", "arbitrary"),
        name="conv_prompt",
    )(x, g, *consts)


def _conv_sample_kernel(x_ref, buf_ref, g_ref, win_ref, bin_ref, dw_ref, dwb_ref, lng_ref, lnb_ref, wout_ref,
                        bout_ref, o_ref, nbuf_ref, *, taps):
    x = x_ref[...]
    d = x.shape[-1]
    hist = taps - 1
    glu = _glu_in(x, g_ref[...], win_ref, bin_ref)
    acc = dwb_ref[...] + dw_ref[hist:taps, :] * glu
    for k in range(hist):
        acc = acc + dw_ref[k:k + 1, :] * buf_ref[:, k * d:(k + 1) * d]
    nbuf_ref[:, 0:(hist - 1) * d] = buf_ref[:, d:hist * d]
    nbuf_ref[:, (hist - 1) * d:hist * d] = glu
    o_ref[...] = _conv_tail(x, acc, lng_ref[...], lnb_ref[...], wout_ref, bout_ref)


def _conv_sample(x, buf, norm_layer, layer, g, win, b_in, dw, dwb, lng, lnb, wout, bout):
    m, d = x.shape
    taps = dw.shape[1]
    tb = SAMPLE_CONV_TILE if m % SAMPLE_CONV_TILE == 0 else m
    row = pl.BlockSpec((tb, d), lambda i: (i, 0))
    brow = pl.BlockSpec((tb, buf.shape[1]), lambda i: (i, 0))
    consts = (win, b_in, dw, dwb, lng, lnb, wout, bout)
    return pl.pallas_call(
        functools.partial(_conv_sample_kernel, taps=taps),
        out_shape=(jax.ShapeDtypeStruct((m, d), F32), jax.ShapeDtypeStruct(buf.shape, F32)),
        grid=(m // tb,),
        in_specs=[row, brow, _layer_spec(g, norm_layer)] + [_layer_spec(c, layer) for c in consts],
        out_specs=(row, brow),
        compiler_params=_cparams("parallel"),
        name="conv_sample",
    )(x, buf, g, *consts)


def _group_rmsnorm_gate(y, z, ng, n_groups):
    y = y * _silu(z)
    gw = y.shape[-1] // n_groups
    outs = []
    for g in range(n_groups):
        s = y[:, g * gw:(g + 1) * gw]
        outs.append(s * lax.rsqrt(jnp.mean(s * s, axis=-1, keepdims=True) + EPS))
    return jnp.concatenate(outs, axis=-1) * ng


def _prefix_sum_rows(a):
    n = a.shape[0]
    rows = lax.broadcasted_iota(jnp.int32, a.shape, 0)
    sh = 1
    while sh < n:
        a = a + jnp.where(rows >= sh, pltpu.roll(a, sh, 0), 0.0)
        sh *= 2
    return a


def _ssd_prompt_kernel(x_ref, g_ref, wz_ref, wxbc_ref, wdt_ref, cw_ref, cb_ref, dtb_ref, alog_ref, dskip_ref,
                       ng_ref, wout_ref, o_ref, nbuf_ref, hfin_ref, slab_ref, hn_ref, z_ref, dt_ref, y_ref, yn_ref,
                       ht_ref, *, n_groups, n_heads, head_dim, d_state, chunk, col_block, row_block):
    t = pl.program_id(1)
    tl = x_ref.shape[0]
    nslab = slab_ref.shape[0]
    pad = slab_ref.shape[1] - tl
    d_inner = n_heads * head_dim
    hpg = n_heads // n_groups
    gw = hpg * head_dim
    ck = cw_ref.shape[0]
    xslabs = d_inner // V7X_LANES
    assert d_state == V7X_LANES and gw % V7X_LANES == 0

    @pl.when(t == 0)
    def _():
        slab_ref[:, 0:pad, :] = jnp.zeros((nslab, pad, V7X_LANES), F32)
        ht_ref[...] = jnp.zeros(ht_ref.shape, F32)

    for r0 in range(0, tl, row_block):
        hn_ref[r0:r0 + row_block, :] = _rmsnorm(x_ref[r0:r0 + row_block, :], g_ref[...]).astype(BF16)
    hn = hn_ref[...]
    dt_ref[...] = _softplus(_dot(hn, wdt_ref[...]) + dtb_ref[...])
    for c0 in range(0, nslab * V7X_LANES, col_block):
        pre = _dot(hn, wxbc_ref[:, c0:c0 + col_block])
        for jj in range(col_block // V7X_LANES):
            j = c0 // V7X_LANES + jj
            ls = slice(j * V7X_LANES, (j + 1) * V7X_LANES)
            slab_ref[j, pad:pad + tl, :] = pre[:, jj * V7X_LANES:(jj + 1) * V7X_LANES]
            tail = slab_ref[j, tl:tl + pad, :]
            for r0 in reversed(range(0, tl, row_block)):
                acc = jnp.broadcast_to(cb_ref[:, ls], (row_block, V7X_LANES))
                for k in range(ck):
                    lo = r0 + pad - (ck - 1) + k
                    acc = acc + cw_ref[k:k + 1, ls] * slab_ref[j, lo:lo + row_block, :]
                slab_ref[j, pad + r0:pad + r0 + row_block, :] = _silu(acc)
            slab_ref[j, 0:pad, :] = tail
    z_ref[...] = _dot(hn, wz_ref[...])

    @pl.when(t == pl.num_programs(1) - 1)
    def _():
        for j in range(nslab):
            nbuf_ref[:, j * V7X_LANES:(j + 1) * V7X_LANES] = slab_ref[j, pad - (ck - 1):pad, :]

    neg_a = jnp.exp(alog_ref[...])
    hps = V7X_LANES // head_dim
    lane = lax.broadcasted_iota(jnp.int32, (1, V7X_LANES), 1)
    causal = lax.broadcasted_iota(jnp.int32, (chunk, chunk), 0) >= lax.broadcasted_iota(jnp.int32, (chunk, chunk), 1)

    def chunk_body(c, carry):
        r0 = pl.multiple_of(c * chunk, chunk)
        rows = pl.ds(pl.multiple_of(pad + c * chunk, V7X_SUBLANES), chunk)
        dt = dt_ref[pl.ds(r0, chunk), :]
        acum = _prefix_sum_rows(-dt * neg_a)
        a_last = acum[chunk - 1:chunk, :]
        acum_t = acum.T
        dt_t = dt.T
        decdt_t = (dt * jnp.exp(a_last - acum)).T
        cdec = jnp.exp(a_last)
        for g in range(n_groups):
            bg = slab_ref[xslabs + g, rows, :]
            cg = slab_ref[xslabs + n_groups + g, rows, :]
            cbm = lax.dot_general(cg.astype(BF16), bg.astype(BF16), (((1,), (1,)), ((), ())),
                                  preferred_element_type=F32)
            bg_t = bg.T
            for sl in range(g * gw // V7X_LANES, (g + 1) * gw // V7X_LANES):
                ls = slice(sl * V7X_LANES, (sl + 1) * V7X_LANES)
                xs_sl = slab_ref[sl, rows, :]
                hprev = ht_ref[:, ls]
                y_sl = upd_sl = cdec_sl = None
                for u in range(hps):
                    h = sl * hps + u
                    mine = (lane >= u * head_dim) & (lane < (u + 1) * head_dim)
                    xm = jnp.where(mine, xs_sl, 0.0).astype(BF16)
                    hm = jnp.where(mine, hprev, 0.0).astype(BF16)
                    colb = jnp.broadcast_to(acum[:, h:h + 1], (chunk, chunk))
                    lmat = jnp.exp(jnp.where(causal, colb - acum_t[h:h + 1, :], -jnp.inf))
                    lhs = jnp.concatenate([cbm * lmat * dt_t[h:h + 1, :], cg * jnp.exp(colb)], axis=1)
                    yh = _dot(lhs.astype(BF16), jnp.concatenate([xm, hm], axis=0))
                    uh = _dot((bg_t * decdt_t[h:h + 1, :]).astype(BF16), xm)
                    cd = jnp.where(mine, cdec[:, h:h + 1], 0.0)
                    y_sl = yh if u == 0 else y_sl + yh
                    upd_sl = uh if u == 0 else upd_sl + uh
                    cdec_sl = cd if u == 0 else cdec_sl + cd
                y_ref[pl.ds(r0, chunk), ls] = y_sl
                ht_ref[:, ls] = hprev * cdec_sl + upd_sl
        return carry

    lax.fori_loop(0, tl // chunk, chunk_body, 0)

    out = x_ref[...]
    for g in range(n_groups):
        gs = slice(g * gw, (g + 1) * gw)
        for r0 in range(0, tl, row_block):
            rs = slice(r0, r0 + row_block)
            xs = jnp.concatenate([slab_ref[j, pad + r0:pad + r0 + row_block, :]
                                  for j in range(g * gw // V7X_LANES, (g + 1) * gw // V7X_LANES)], axis=1)
            yg = (y_ref[rs, gs] + dskip_ref[:, gs] * xs) * _silu(z_ref[rs, gs])
            yg = yg * lax.rsqrt(jnp.mean(yg * yg, axis=-1, keepdims=True) + EPS) * ng_ref[:, gs]
            yn_ref[rs, gs] = yg.astype(BF16)
        out = out + _dot(yn_ref[:, gs], wout_ref[gs, :])
    o_ref[...] = out

    @pl.when(t == pl.num_programs(1) - 1)
    def _():
        hfin_ref[...] = ht_ref[...].T


def _ssd_prompt(x, norm_layer, layer, g, wz, wxbc, wdt, cw, cb, dtb, alog, dskip, ng, wout, dims):
    b, l, d = x.shape
    n_groups, n_heads, head_dim, d_state = dims
    d_inner = n_heads * head_dim
    cd = wxbc.shape[2]
    ck = cw.shape[1]
    tl = SSD_TILE if l % SSD_TILE == 0 else l
    q = SSD_CHUNK if tl % SSD_CHUNK == 0 else tl
    row = pl.BlockSpec((None, tl, d), lambda i, j: (i, j, 0))
    consts = (wz, wxbc, wdt, cw, cb, dtb, alog, dskip, ng, wout)
    return pl.pallas_call(
        functools.partial(_ssd_prompt_kernel, n_groups=n_groups, n_heads=n_heads, head_dim=head_dim,
                          d_state=d_state, chunk=q, col_block=min(cd, 2 * V7X_LANES), row_block=q),
        out_shape=(jax.ShapeDtypeStruct((b, l, d), F32),
                   jax.ShapeDtypeStruct((b, ck - 1, cd), F32),
                   jax.ShapeDtypeStruct((b, d_inner, d_state), F32)),
        grid=(b, l // tl),
        in_specs=[row, _layer_spec(g, norm_layer)] + [_layer_spec(c, layer) for c in consts],
        out_specs=(row,
                   pl.BlockSpec((None, ck - 1, cd), lambda i, j: (i, 0, 0)),
                   pl.BlockSpec((None, d_inner, d_state), lambda i, j: (i, 0, 0))),
        scratch_shapes=[pltpu.VMEM((cd // V7X_LANES, tl + V7X_SUBLANES, V7X_LANES), F32),
                        pltpu.VMEM((tl, d), BF16),
                        pltpu.VMEM((tl, d_inner), F32),
                        pltpu.VMEM((tl, V7X_LANES), F32),
                        pltpu.VMEM((tl, d_inner), F32),
                        pltpu.VMEM((tl, d_inner), BF16),
                        pltpu.VMEM((d_state, d_inner), F32)],
        compiler_params=_cparams("parallel", "arbitrary"),
        name="ssd_prompt",
    )(x, g, *consts)


def _expand_heads(v, e_ref):
    hi = v.astype(BF16)
    r1 = v - hi.astype(F32)
    mid = r1.astype(BF16)
    lo = (r1 - mid.astype(F32)).astype(BF16)
    e = e_ref[...]
    return _dot(hi, e) + _dot(mid, e) + _dot(lo, e)


def _ssd_sample_in_kernel(x_ref, buf_ref, g_ref, wz_ref, wxbc_ref, wdt_ref, cw_ref, cb_ref, dtb_ref, alog_ref,
                          e_ref, z_ref, xs_ref, bm_ref, cmt_ref, xdtt_ref, expa_ref, nbuf_ref,
                          *, d_inner, gn):
    x = x_ref[...]
    hn = _rmsnorm(x, g_ref[...]).astype(BF16)
    z_ref[...] = _dot(hn, wz_ref[...])
    new = _dot(hn, wxbc_ref[...])
    cd = new.shape[1]
    ck = cw_ref.shape[0]
    xbc = cb_ref[...] + cw_ref[ck - 1:ck, :] * new
    for k in range(ck - 1):
        xbc = xbc + cw_ref[k:k + 1, :] * buf_ref[:, k * cd:(k + 1) * cd]
    nbuf_ref[:, 0:(ck - 2) * cd] = buf_ref[:, cd:(ck - 1) * cd]
    nbuf_ref[:, (ck - 2) * cd:(ck - 1) * cd] = new
    xbc = _silu(xbc)
    xs = xbc[:, :d_inner]
    dt = _softplus(_dot(hn, wdt_ref[...]) + dtb_ref[...])
    xs_ref[...] = xs
    bm_ref[...] = xbc[:, d_inner:d_inner + gn]
    cmt_ref[...] = xbc[:, d_inner + gn:].T
    xdtt_ref[...] = (xs * _expand_heads(dt, e_ref)).T
    expa_ref[...] = jnp.exp(dt * (-jnp.exp(alog_ref[...])))


def _ssd_sample_state_kernel(h0_ref, bm_ref, cmt_ref, xdtt_ref, expa_ref, *rest,
                             n_groups, n_heads, head_dim, d_state, layer):
    hout_ref, yt_ref = rest[-2:]
    t = pl.program_id(0)
    tb = h0_ref.shape[0]
    ntok = bm_ref.shape[0]
    hpg = n_heads // n_groups
    gw = hpg * head_dim
    if len(hout_ref.shape) == 4:
        for other in range(hout_ref.shape[0]):
            if other != layer:
                hout_ref[other] = jnp.zeros(hout_ref.shape[1:], F32)
        hout_ref = hout_ref.at[layer]

    @pl.when(t == 0)
    def _():
        yt_ref[...] = jnp.zeros(yt_ref.shape, F32)

    tok_rows = lax.broadcasted_iota(jnp.int32, (ntok, d_state), 0)
    tok_cols = lax.broadcasted_iota(jnp.int32, (d_state, ntok), 1)

    def body(i, carry):
        tok = t * tb + i
        ea = expa_ref[pl.ds(tok, 1), :]
        for g in range(n_groups):
            rs = slice(g * gw, (g + 1) * gw)
            ns = slice(g * d_state, (g + 1) * d_state)
            rb = jnp.where(tok_rows == tok, bm_ref[:, ns], 0.0).astype(BF16)
            upd = _dot(xdtt_ref[rs, :].astype(BF16), rb)
            parts = []
            for r in range(hpg):
                h = g * hpg + r
                hr = slice(h * head_dim, (h + 1) * head_dim)
                parts.append(h0_ref[i, hr, :] * ea[:, h:h + 1])
            hnew = jnp.concatenate(parts, axis=0) + upd
            hout_ref[i, rs, :] = hnew
            rc = jnp.where(tok_cols == tok, cmt_ref[ns, :], 0.0).astype(BF16)
            yt_ref[rs, :] = yt_ref[rs, :] + _dot(hnew.astype(BF16), rc)
        return carry

    lax.fori_loop(0, tb, body, 0)


def _ssd_sample_out_kernel(x_ref, yt_ref, xs_ref, z_ref, dskip_ref, ng_ref, wout_ref, o_ref, *, n_groups):
    y = yt_ref[...].T + dskip_ref[...] * xs_ref[...]
    yn = _group_rmsnorm_gate(y, z_ref[...], ng_ref[...], n_groups).astype(BF16)
    o_ref[...] = x_ref[...] + _dot(yn, wout_ref[...])


def _ssd_sample(x, buf, h0_all, hstack, norm_layer, layer, g, wz, wxbc, wdt, cw, cb, dtb, alog, dskip, ng, wout, e,
                dims):
    m, d = x.shape
    n_groups, n_heads, head_dim, d_state = dims
    d_inner = n_heads * head_dim
    gn = n_groups * d_state
    consts = (wz, wxbc, wdt, cw, cb, dtb, alog)
    full = lambda a: pl.BlockSpec(a.shape, lambda *_: (0,) * a.ndim)
    outs = (jax.ShapeDtypeStruct((m, d_inner), F32),
            jax.ShapeDtypeStruct((m, d_inner), F32),
            jax.ShapeDtypeStruct((m, gn), F32),
            jax.ShapeDtypeStruct((gn, m), F32),
            jax.ShapeDtypeStruct((d_inner, m), F32),
            jax.ShapeDtypeStruct((m, V7X_LANES), F32),
            jax.ShapeDtypeStruct(buf.shape, F32))
    z, xs, bm, cmt, xdtt, expa, nbuf = pl.pallas_call(
        functools.partial(_ssd_sample_in_kernel, d_inner=d_inner, gn=gn),
        out_shape=outs,
        grid=(1,),
        in_specs=[full(x), full(buf), _layer_spec(g, norm_layer)] + [_layer_spec(c, layer) for c in consts]
        + [_const_spec(e)],
        out_specs=tuple(pl.BlockSpec(o.shape, lambda *_, n=len(o.shape): (0,) * n) for o in outs),
        compiler_params=_cparams("arbitrary"),
        name="ssd_sample_in",
    )(x, buf, g, *consts, e)

    first = hstack is None
    tile = SAMPLE_STATE_TILE // (h0_all.shape[0] if first else 1)
    tb = tile if tile and m % tile == 0 else m
    st = pl.BlockSpec((None, tb, d_inner, d_state), lambda i: (layer, i, 0, 0))
    ins = [h0_all, bm, cmt, xdtt, expa]
    in_specs = [st, _const_spec(bm), _const_spec(cmt), _const_spec(xdtt), _const_spec(expa)]
    aliases = {}
    if first:
        st_out = pl.BlockSpec((h0_all.shape[0], tb, d_inner, d_state), lambda i: (0, i, 0, 0))
    else:
        st_out = st
        aliases = {len(ins): 0}
        ins.append(hstack)
        in_specs.append(pl.BlockSpec(memory_space=pl.ANY))
    hstack, yt = pl.pallas_call(
        functools.partial(_ssd_sample_state_kernel, n_groups=n_groups, n_heads=n_heads, head_dim=head_dim,
                          d_state=d_state, layer=layer),
        out_shape=(jax.ShapeDtypeStruct(h0_all.shape, F32), jax.ShapeDtypeStruct((d_inner, m), F32)),
        grid=(m // tb,),
        in_specs=in_specs,
        out_specs=(st_out, pl.BlockSpec((d_inner, m), lambda i: (0, 0))),
        input_output_aliases=aliases,
        compiler_params=_cparams("arbitrary"),
        name="ssd_sample_state",
    )(*ins)

    out = pl.pallas_call(
        functools.partial(_ssd_sample_out_kernel, n_groups=n_groups),
        out_shape=jax.ShapeDtypeStruct((m, d), F32),
        grid=(1,),
        in_specs=[full(x), full(yt), full(xs), full(z), _layer_spec(dskip, layer), _layer_spec(ng, layer),
                  _layer_spec(wout, layer)],
        out_specs=full(x),
        compiler_params=_cparams("arbitrary"),
        name="ssd_sample_out",
    )(x, yt, xs, z, dskip, ng, wout)
    return out, nbuf, hstack


def _rows(v):
    return v.reshape(v.shape[0], 1, v.shape[1]).astype(F32)


def _pad_lanes(v):
    return jnp.pad(v, [(0, 0)] * (v.ndim - 1) + [(0, V7X_LANES - v.shape[-1])])


def kernel(x_prompt, x_sample, state_conv, state_ssd_conv, state_ssd, p_prompt, p_sample, norm_ffn1, w_ffn1_gate, w_ffn1_up, w_ffn1_down, norm_mix, norm_ffn2, w_ffn2_gate, w_ffn2_up, w_ffn2_down, norm_ple, w_ple_gate, w_ple_proj, cm_w_in, cm_b_in, cm_dw, cm_dw_b, cm_ln_g, cm_ln_b, cm_w_out, cm_b_out, ssd_w_in, ssd_conv_w, ssd_conv_b, ssd_dt_bias, ssd_A_log, ssd_D, ssd_norm, ssd_w_out, final_norm):
    depth = norm_ffn1.shape[0]
    bp, lp, d = x_prompt.shape
    bs = x_sample.shape[0]
    n_heads = ssd_dt_bias.shape[1]
    head_dim, d_state = state_ssd.shape[3], state_ssd.shape[4]
    d_inner = n_heads * head_dim
    conv_dim = ssd_conv_w.shape[2]
    n_groups = (conv_dim - d_inner) // (2 * d_state)
    dims = (n_groups, n_heads, head_dim, d_state)
    assert n_heads <= V7X_LANES and lp % SSD_CHUNK == 0 and x_sample.shape[1] == 1

    bf = lambda w: w.astype(BF16)
    ffn1 = (_rows(norm_ffn1), bf(w_ffn1_gate), bf(w_ffn1_up), bf(w_ffn1_down))
    tail = (_rows(norm_ffn2), bf(w_ffn2_gate), bf(w_ffn2_up), bf(w_ffn2_down), _rows(norm_ple), bf(w_ple_gate),
            bf(w_ple_proj))
    fnorm = final_norm.reshape(1, d)
    nmix = _rows(norm_mix)
    cmw = (bf(cm_w_in), _rows(cm_b_in), cm_dw, _rows(cm_dw_b), _rows(cm_ln_g), _rows(cm_ln_b), bf(cm_w_out),
           _rows(cm_b_out))
    ssw = (bf(ssd_w_in[:, :, :d_inner]), bf(ssd_w_in[:, :, d_inner:d_inner + conv_dim]),
           bf(_pad_lanes(ssd_w_in[:, :, d_inner + conv_dim:])), ssd_conv_w, _rows(ssd_conv_b),
           _rows(_pad_lanes(ssd_dt_bias)), _rows(_pad_lanes(ssd_A_log)),
           _rows(jnp.repeat(ssd_D, head_dim, axis=1)), _rows(ssd_norm), bf(ssd_w_out))
    expand = (lax.broadcasted_iota(jnp.int32, (V7X_LANES, d_inner), 0)
              == lax.broadcasted_iota(jnp.int32, (V7X_LANES, d_inner), 1) // head_dim).astype(BF16)

    xp = x_prompt.reshape(bp * lp, d)
    xsm = x_sample.reshape(bs, d)
    pp = p_prompt.reshape(depth, bp * lp, -1)
    ps = p_sample.reshape(depth, bs, -1)
    h0_all = state_ssd.reshape(state_ssd.shape[0], bs, d_inner, d_state)
    conv_p, xbc_p, ssm_p, conv_s, xbc_s = [], [], [], [], []
    ssm_s = None

    for i in range(depth):
        j = i // 2
        xp = _ffn(xp, i, *ffn1)
        xsm = _ffn(xsm, i, *ffn1)
        if i % 2 == 0:
            xp3, nb = _conv_prompt(xp.reshape(bp, lp, d), i, j, nmix, *cmw)
            xp = xp3.reshape(bp * lp, d)
            conv_p.append(nb)
            xsm, nb = _conv_sample(xsm, state_conv[j].reshape(bs, -1), i, j, nmix, *cmw)
            conv_s.append(nb.reshape(state_conv.shape[1:]))
        else:
            xp3, nb, hf = _ssd_prompt(xp.reshape(bp, lp, d), i, j, nmix, *ssw, dims)
            xp = xp3.reshape(bp * lp, d)
            xbc_p.append(nb)
            ssm_p.append(hf.reshape(bp, n_heads, head_dim, d_state))
            xsm, nb, ssm_s = _ssd_sample(xsm, state_ssd_conv[j].reshape(bs, -1), h0_all, ssm_s, i, j, nmix, *ssw,
                                         expand, dims)
            xbc_s.append(nb.reshape(state_ssd_conv.shape[1:]))
        last = i == depth - 1
        xp = _ffn_ple(xp, pp, i, *tail, fnorm, final_norm=last)
        xsm = _ffn_ple(xsm, ps, i, *tail, fnorm, final_norm=last)

    return (xp.reshape(bp, lp, d), xsm.reshape(bs, 1, d),
            jnp.stack(conv_p), jnp.stack(xbc_p), jnp.stack(ssm_p),
            jnp.stack(conv_s), jnp.stack(xbc_s), ssm_s.reshape(state_ssd.shape))
```

```python
import functools

import jax
import jax.numpy as jnp
from jax import lax
from jax.experimental import pallas as pl
from jax.experimental.pallas import tpu as pltpu

F32 = jnp.float32
BF16 = jnp.bfloat16
EPS = 1e-6

V7X_LANES = 128
V7X_SUBLANES = 8
V7X_VMEM_LIMIT_BYTES = 56 * 1024 * 1024

SSD_CHUNK = 128
SSD_TILE = 512
TOKEN_TILE = 512
CONV_TILE = 512
CONV_ROWS = 128
SAMPLE_CONV_TILE = 32
SAMPLE_STATE_TILE = 8


def _cparams(*sem):
    return pltpu.CompilerParams(dimension_semantics=sem, vmem_limit_bytes=V7X_VMEM_LIMIT_BYTES)


def _const_spec(arr):
    nd = arr.ndim
    return pl.BlockSpec(arr.shape, lambda *_: (0,) * nd, pipeline_mode=pl.Buffered(1))


def _layer_spec(arr, layer):
    nd = arr.ndim
    return pl.BlockSpec((None,) + arr.shape[1:], lambda *_: (layer,) + (0,) * (nd - 1),
                        pipeline_mode=pl.Buffered(1))


def _dot(a, b):
    return jnp.dot(a, b, preferred_element_type=F32)


def _rmsnorm(x, g):
    return x * lax.rsqrt(jnp.mean(x * x, axis=-1, keepdims=True) + EPS) * g


def _layernorm(x, g, b):
    mu = jnp.mean(x, axis=-1, keepdims=True)
    xc = x - mu
    var = jnp.mean(xc * xc, axis=-1, keepdims=True)
    return xc * lax.rsqrt(var + EPS) * g + b


def _sigmoid(x):
    return 0.5 * jnp.tanh(0.5 * x) + 0.5


def _silu(x):
    h = 0.5 * x
    return h * jnp.tanh(h) + h


def _softplus(x):
    return jnp.maximum(x, 0.0) + jnp.log(1.0 + jnp.exp(-jnp.abs(x)))


def _swiglu(x, g, wg_ref, wu_ref, wd_ref):
    h = _rmsnorm(x, g).astype(BF16)
    gate = _dot(h, wg_ref[...])
    up = _dot(h, wu_ref[...])
    act = (_silu(gate) * up).astype(BF16)
    return _dot(act, wd_ref[...])


def _ffn_kernel(x_ref, g_ref, wg_ref, wu_ref, wd_ref, o_ref):
    x = x_ref[...]
    o_ref[...] = x + 0.5 * _swiglu(x, g_ref[...], wg_ref, wu_ref, wd_ref)


def _ffn_ple_kernel(x_ref, p_ref, g_ref, wg_ref, wu_ref, wd_ref, gp_ref, wpg_ref, wpp_ref, fn_ref, o_ref,
                    *, final_norm):
    x = x_ref[...]
    x = x + 0.5 * _swiglu(x, g_ref[...], wg_ref, wu_ref, wd_ref)
    h = _rmsnorm(x, gp_ref[...]).astype(BF16)
    gate = _sigmoid(_dot(h, wpg_ref[...]))
    x = x + gate * _dot(p_ref[...].astype(BF16), wpp_ref[...])
    if final_norm:
        x = _rmsnorm(x, fn_ref[...])
    o_ref[...] = x


def _token_tile(m):
    return TOKEN_TILE if m % TOKEN_TILE == 0 else m


def _ffn(x, layer, g, wg, wu, wd):
    m, d = x.shape
    tm = _token_tile(m)
    row = pl.BlockSpec((tm, d), lambda i: (i, 0))
    consts = (g, wg, wu, wd)
    return pl.pallas_call(
        _ffn_kernel,
        out_shape=jax.ShapeDtypeStruct((m, d), F32),
        grid=(m // tm,),
        in_specs=[row] + [_layer_spec(c, layer) for c in consts],
        out_specs=row,
        compiler_params=_cparams("parallel"),
        name="ffn",
    )(x, *consts)


def _ffn_ple(x, p, layer, g, wg, wu, wd, gp, wpg, wpp, fn, final_norm):
    m, d = x.shape
    tm = _token_tile(m)
    row = pl.BlockSpec((tm, d), lambda i: (i, 0))
    prow = pl.BlockSpec((None, tm, p.shape[2]), lambda i: (layer, i, 0))
    consts = (g, wg, wu, wd, gp, wpg, wpp)
    return pl.pallas_call(
        functools.partial(_ffn_ple_kernel, final_norm=final_norm),
        out_shape=jax.ShapeDtypeStruct((m, d), F32),
        grid=(m // tm,),
        in_specs=[row, prow] + [_layer_spec(c, layer) for c in consts] + [_const_spec(fn)],
        out_specs=row,
        compiler_params=_cparams("parallel"),
        name="ffn_ple",
    )(x, p, *consts, fn)


def _glu_in(x, g, win_ref, bin_ref):
    d = x.shape[-1]
    u = _dot(_rmsnorm(x, g).astype(BF16), win_ref[...]) + bin_ref[...]
    return u[:, :d] * _sigmoid(u[:, d:])


def _conv_tail(x, v, lng, lnb, wout_ref, bout_ref):
    v = _silu(_layernorm(v, lng, lnb)).astype(BF16)
    return x + _dot(v, wout_ref[...]) + bout_ref[...]


def _conv_prompt_kernel(x_ref, g_ref, win_ref, bin_ref, dw_ref, dwb_ref, lng_ref, lnb_ref, wout_ref, bout_ref,
                        o_ref, nbuf_ref, slab_ref, v_ref, *, taps, row_block):
    t = pl.program_id(1)
    tl = x_ref.shape[0]
    nslab = slab_ref.shape[0]
    pad = slab_ref.shape[1] - tl
    hist = taps - 1

    @pl.when(t == 0)
    def _():
        slab_ref[:, 0:pad, :] = jnp.zeros((nslab, pad, V7X_LANES), F32)

    x = x_ref[...]
    glu = _glu_in(x, g_ref[...], win_ref, bin_ref)
    for j in range(nslab):
        slab_ref[j, pad:pad + tl, :] = glu[:, j * V7X_LANES:(j + 1) * V7X_LANES]

    for j in range(nslab):
        ls = slice(j * V7X_LANES, (j + 1) * V7X_LANES)
        for r0 in range(0, tl, row_block):
            acc = jnp.broadcast_to(dwb_ref[:, ls], (row_block, V7X_LANES))
            for k in range(taps):
                lo = r0 + pad - hist + k
                acc = acc + dw_ref[k:k + 1, ls] * slab_ref[j, lo:lo + row_block, :]
            v_ref[r0:r0 + row_block, ls] = acc
    o_ref[...] = _conv_tail(x, v_ref[...], lng_ref[...], lnb_ref[...], wout_ref, bout_ref)

    @pl.when(t == pl.num_programs(1) - 1)
    def _():
        for j in range(nslab):
            nbuf_ref[:, j * V7X_LANES:(j + 1) * V7X_LANES] = slab_ref[j, pad + tl - hist:pad + tl, :]

    slab_ref[:, 0:pad, :] = slab_ref[:, tl:tl + pad, :]


def _conv_prompt(x, norm_layer, layer, g, win, b_in, dw, dwb, lng, lnb, wout, bout):
    b, l, d = x.shape
    taps = dw.shape[1]
    tl = CONV_TILE if l % CONV_TILE == 0 else l
    pad = -(-(taps - 1) // V7X_SUBLANES) * V7X_SUBLANES
    row = pl.BlockSpec((None, tl, d), lambda i, j: (i, j, 0))
    consts = (win, b_in, dw, dwb, lng, lnb, wout, bout)
    return pl.pallas_call(
        functools.partial(_conv_prompt_kernel, taps=taps, row_block=min(tl, CONV_ROWS)),
        out_shape=(jax.ShapeDtypeStruct((b, l, d), F32), jax.ShapeDtypeStruct((b, taps - 1, d), F32)),
        grid=(b, l // tl),
        in_specs=[row, _layer_spec(g, norm_layer)] + [_layer_spec(c, layer) for c in consts],
        out_specs=(row, pl.BlockSpec((None, taps - 1, d), lambda i, j: (i, 0, 0))),
        scratch_shapes=[pltpu.VMEM((d // V7X_LANES, tl + pad, V7X_LANES), F32), pltpu.VMEM((tl, d), F32)],
        compiler_params=_cparams("parallel", "arbitrary"),
        name="conv_prompt",
    )(x, g, *consts)


def _conv_sample_kernel(x_ref, buf_ref, g_ref, win_ref, bin_ref, dw_ref, dwb_ref, lng_ref, lnb_ref, wout_ref,
                        bout_ref, *rest, taps, layer):
    o_ref, nbuf_ref = rest[-2:]
    if len(nbuf_ref.shape) == 4:
        for other in range(nbuf_ref.shape[0]):
            if other != layer:
                nbuf_ref[other] = jnp.zeros(nbuf_ref.shape[1:], F32)
        nbuf_ref = nbuf_ref.at[layer]
    x = x_ref[...]
    hist = taps - 1
    glu = _glu_in(x, g_ref[...], win_ref, bin_ref)
    acc = dwb_ref[...] + dw_ref[hist:taps, :] * glu
    for k in range(hist):
        acc = acc + dw_ref[k:k + 1, :] * buf_ref[:, k, :]
    for k in range(hist - 1):
        nbuf_ref[:, k, :] = buf_ref[:, k + 1, :]
    nbuf_ref[:, hist - 1, :] = glu
    o_ref[...] = _conv_tail(x, acc, lng_ref[...], lnb_ref[...], wout_ref, bout_ref)


def _conv_sample(x, buf_all, stack, norm_layer, layer, g, win, b_in, dw, dwb, lng, lnb, wout, bout):
    m, d = x.shape
    taps = dw.shape[1]
    n_layers, _, hist, _ = buf_all.shape
    first = stack is None
    tile = SAMPLE_CONV_TILE // (n_layers if first else 1)
    tb = tile if tile and m % tile == 0 else m
    row = pl.BlockSpec((tb, d), lambda i: (i, 0))
    brow = pl.BlockSpec((None, tb, hist, d), lambda i: (layer, i, 0, 0))
    consts = (win, b_in, dw, dwb, lng, lnb, wout, bout)
    ins = [x, buf_all, g, *consts]
    in_specs = [row, brow, _layer_spec(g, norm_layer)] + [_layer_spec(c, layer) for c in consts]
    aliases = {}
    if first:
        srow = pl.BlockSpec((n_layers, tb, hist, d), lambda i: (0, i, 0, 0))
    else:
        srow = brow
        aliases = {len(ins): 1}
        ins.append(stack)
        in_specs.append(pl.BlockSpec(memory_space=pl.ANY))
    return pl.pallas_call(
        functools.partial(_conv_sample_kernel, taps=taps, layer=layer),
        out_shape=(jax.ShapeDtypeStruct((m, d), F32), jax.ShapeDtypeStruct(buf_all.shape, F32)),
        grid=(m // tb,),
        in_specs=in_specs,
        out_specs=(row, srow),
        input_output_aliases=aliases,
        compiler_params=_cparams("arbitrary"),
        name="conv_sample",
    )(*ins)


def _group_rmsnorm_gate(y, z, ng, n_groups):
    y = y * _silu(z)
    gw = y.shape[-1] // n_groups
    outs = []
    for g in range(n_groups):
        s = y[:, g * gw:(g + 1) * gw]
        outs.append(s * lax.rsqrt(jnp.mean(s * s, axis=-1, keepdims=True) + EPS))
    return jnp.concatenate(outs, axis=-1) * ng


def _prefix_sum_rows(a):
    n = a.shape[0]
    rows = lax.broadcasted_iota(jnp.int32, a.shape, 0)
    sh = 1
    while sh < n:
        a = a + jnp.where(rows >= sh, pltpu.roll(a, sh, 0), 0.0)
        sh *= 2
    return a


def _ssd_prompt_kernel(x_ref, g_ref, wz_ref, wxbc_ref, wdt_ref, cw_ref, cb_ref, dtb_ref, alog_ref, dskip_ref,
                       ng_ref, wout_ref, o_ref, nbuf_ref, hfin_ref, slab_ref, hn_ref, z_ref, dt_ref, y_ref, yn_ref,
                       ht_ref, *, n_groups, n_heads, head_dim, d_state, chunk, col_block, row_block):
    t = pl.program_id(1)
    tl = x_ref.shape[0]
    nslab = slab_ref.shape[0]
    pad = slab_ref.shape[1] - tl
    d_inner = n_heads * head_dim
    hpg = n_heads // n_groups
    gw = hpg * head_dim
    ck = cw_ref.shape[0]
    xslabs = d_inner // V7X_LANES
    assert d_state == V7X_LANES and gw % V7X_LANES == 0

    @pl.when(t == 0)
    def _():
        slab_ref[:, 0:pad, :] = jnp.zeros((nslab, pad, V7X_LANES), F32)
        ht_ref[...] = jnp.zeros(ht_ref.shape, F32)

    for r0 in range(0, tl, row_block):
        hn_ref[r0:r0 + row_block, :] = _rmsnorm(x_ref[r0:r0 + row_block, :], g_ref[...]).astype(BF16)
    hn = hn_ref[...]
    dt_ref[...] = _softplus(_dot(hn, wdt_ref[...]) + dtb_ref[...])
    for c0 in range(0, nslab * V7X_LANES, col_block):
        pre = _dot(hn, wxbc_ref[:, c0:c0 + col_block])
        for jj in range(col_block // V7X_LANES):
            slab_ref[c0 // V7X_LANES + jj, pad:pad + tl, :] = pre[:, jj * V7X_LANES:(jj + 1) * V7X_LANES]
    z_ref[...] = _dot(hn, wz_ref[...])
    for j in range(nslab):
        ls = slice(j * V7X_LANES, (j + 1) * V7X_LANES)
        tail = slab_ref[j, tl:tl + pad, :]
        for r0 in reversed(range(0, tl, row_block)):
            acc = jnp.broadcast_to(cb_ref[:, ls], (row_block, V7X_LANES))
            for k in range(ck):
                lo = r0 + pad - (ck - 1) + k
                acc = acc + cw_ref[k:k + 1, ls] * slab_ref[j, lo:lo + row_block, :]
            slab_ref[j, pad + r0:pad + r0 + row_block, :] = _silu(acc)
        slab_ref[j, 0:pad, :] = tail

    @pl.when(t == pl.num_programs(1) - 1)
    def _():
        for j in range(nslab):
            nbuf_ref[:, j * V7X_LANES:(j + 1) * V7X_LANES] = slab_ref[j, pad - (ck - 1):pad, :]

    neg_a = jnp.exp(alog_ref[...])
    hps = V7X_LANES // head_dim
    lane = lax.broadcasted_iota(jnp.int32, (1, V7X_LANES), 1)
    causal = lax.broadcasted_iota(jnp.int32, (chunk, chunk), 0) >= lax.broadcasted_iota(jnp.int32, (chunk, chunk), 1)

    def chunk_body(c, carry):
        r0 = pl.multiple_of(c * chunk, chunk)
        rows = pl.ds(pl.multiple_of(pad + c * chunk, V7X_SUBLANES), chunk)
        dt = dt_ref[pl.ds(r0, chunk), :]
        acum = _prefix_sum_rows(-dt * neg_a)
        a_last = acum[chunk - 1:chunk, :]
        acum_t = acum.T
        dt_t = dt.T
        decdt_t = (dt * jnp.exp(a_last - acum)).T
        cdec = jnp.exp(a_last)
        for g in range(n_groups):
            bg = slab_ref[xslabs + g, rows, :]
            cg = slab_ref[xslabs + n_groups + g, rows, :]
            cbm = lax.dot_general(cg.astype(BF16), bg.astype(BF16), (((1,), (1,)), ((), ())),
                                  preferred_element_type=F32)
            bg_t = bg.T
            for sl in range(g * gw // V7X_LANES, (g + 1) * gw // V7X_LANES):
                ls = slice(sl * V7X_LANES, (sl + 1) * V7X_LANES)
                xs_sl = slab_ref[sl, rows, :]
                hprev = ht_ref[:, ls]
                y_sl = upd_sl = cdec_sl = None
                for u in range(hps):
                    h = sl * hps + u
                    mine = (lane >= u * head_dim) & (lane < (u + 1) * head_dim)
                    xm = jnp.where(mine, xs_sl, 0.0).astype(BF16)
                    hm = jnp.where(mine, hprev, 0.0).astype(BF16)
                    colb = jnp.broadcast_to(acum[:, h:h + 1], (chunk, chunk))
                    lmat = jnp.exp(jnp.where(causal, colb - acum_t[h:h + 1, :], -jnp.inf))
                    lhs = jnp.concatenate([cbm * lmat * dt_t[h:h + 1, :], cg * jnp.exp(colb)], axis=1)
                    yh = _dot(lhs.astype(BF16), jnp.concatenate([xm, hm], axis=0))
                    uh = _dot((bg_t * decdt_t[h:h + 1, :]).astype(BF16), xm)
                    cd = jnp.where(mine, cdec[:, h:h + 1], 0.0)
                    y_sl = yh if u == 0 else y_sl + yh
                    upd_sl = uh if u == 0 else upd_sl + uh
                    cdec_sl = cd if u == 0 else cdec_sl + cd
                y_ref[pl.ds(r0, chunk), ls] = y_sl
                ht_ref[:, ls] = hprev * cdec_sl + upd_sl
        return carry

    lax.fori_loop(0, tl // chunk, chunk_body, 0)

    out = x_ref[...]
    for g in range(n_groups):
        gs = slice(g * gw, (g + 1) * gw)
        for r0 in range(0, tl, row_block):
            rs = slice(r0, r0 + row_block)
            xs = jnp.concatenate([slab_ref[j, pad + r0:pad + r0 + row_block, :]
                                  for j in range(g * gw // V7X_LANES, (g + 1) * gw // V7X_LANES)], axis=1)
            yg = (y_ref[rs, gs] + dskip_ref[:, gs] * xs) * _silu(z_ref[rs, gs])
            yg = yg * lax.rsqrt(jnp.mean(yg * yg, axis=-1, keepdims=True) + EPS) * ng_ref[:, gs]
            yn_ref[rs, gs] = yg.astype(BF16)
        out = out + _dot(yn_ref[:, gs], wout_ref[gs, :])
    o_ref[...] = out

    @pl.when(t == pl.num_programs(1) - 1)
    def _():
        hfin_ref[...] = ht_ref[...].T


def _ssd_prompt(x, norm_layer, layer, g, wz, wxbc, wdt, cw, cb, dtb, alog, dskip, ng, wout, dims):
    b, l, d = x.shape
    n_groups, n_heads, head_dim, d_state = dims
    d_inner = n_heads * head_dim
    cd = wxbc.shape[2]
    ck = cw.shape[1]
    tl = SSD_TILE if l % SSD_TILE == 0 else l
    q = SSD_CHUNK if tl % SSD_CHUNK == 0 else tl
    row = pl.BlockSpec((None, tl, d), lambda i, j: (i, j, 0))
    consts = (wz, wxbc, wdt, cw, cb, dtb, alog, dskip, ng, wout)
    return pl.pallas_call(
        functools.partial(_ssd_prompt_kernel, n_groups=n_groups, n_heads=n_heads, head_dim=head_dim,
                          d_state=d_state, chunk=q, col_block=min(cd, 2 * V7X_LANES), row_block=q),
        out_shape=(jax.ShapeDtypeStruct((b, l, d), F32),
                   jax.ShapeDtypeStruct((b, ck - 1, cd), F32),
                   jax.ShapeDtypeStruct((b, d_inner, d_state), F32)),
        grid=(b, l // tl),
        in_specs=[row, _layer_spec(g, norm_layer)] + [_layer_spec(c, layer) for c in consts],
        out_specs=(row,
                   pl.BlockSpec((None, ck - 1, cd), lambda i, j: (i, 0, 0)),
                   pl.BlockSpec((None, d_inner, d_state), lambda i, j: (i, 0, 0))),
        scratch_shapes=[pltpu.VMEM((cd // V7X_LANES, tl + V7X_SUBLANES, V7X_LANES), F32),
                        pltpu.VMEM((tl, d), BF16),
                        pltpu.VMEM((tl, d_inner), F32),
                        pltpu.VMEM((tl, V7X_LANES), F32),
                        pltpu.VMEM((tl, d_inner), F32),
                        pltpu.VMEM((tl, d_inner), BF16),
                        pltpu.VMEM((d_state, d_inner), F32)],
        compiler_params=_cparams("parallel", "arbitrary"),
        name="ssd_prompt",
    )(x, g, *consts)


def _expand_heads(v, e_ref):
    hi = v.astype(BF16)
    r1 = v - hi.astype(F32)
    mid = r1.astype(BF16)
    lo = (r1 - mid.astype(F32)).astype(BF16)
    e = e_ref[...]
    return _dot(hi, e) + _dot(mid, e) + _dot(lo, e)


def _ssd_sample_in_kernel(x_ref, buf_ref, g_ref, wz_ref, wxbc_ref, wdt_ref, cw_ref, cb_ref, dtb_ref, alog_ref,
                          e_ref, z_ref, xs_ref, bm_ref, cmt_ref, xdtt_ref, expa_ref, nbuf_ref,
                          *, d_inner, gn):
    x = x_ref[...]
    hn = _rmsnorm(x, g_ref[...]).astype(BF16)
    z_ref[...] = _dot(hn, wz_ref[...])
    new = _dot(hn, wxbc_ref[...])
    cd = new.shape[1]
    ck = cw_ref.shape[0]
    xbc = cb_ref[...] + cw_ref[ck - 1:ck, :] * new
    for k in range(ck - 1):
        xbc = xbc + cw_ref[k:k + 1, :] * buf_ref[:, k * cd:(k + 1) * cd]
    nbuf_ref[:, 0:(ck - 2) * cd] = buf_ref[:, cd:(ck - 1) * cd]
    nbuf_ref[:, (ck - 2) * cd:(ck - 1) * cd] = new
    xbc = _silu(xbc)
    xs = xbc[:, :d_inner]
    dt = _softplus(_dot(hn, wdt_ref[...]) + dtb_ref[...])
    xs_ref[...] = xs
    bm_ref[...] = xbc[:, d_inner:d_inner + gn]
    cmt_ref[...] = xbc[:, d_inner + gn:].T
    xdtt_ref[...] = (xs * _expand_heads(dt, e_ref)).T
    expa_ref[...] = jnp.exp(dt * (-jnp.exp(alog_ref[...])))


def _ssd_sample_state_kernel(h0_ref, bm_ref, cmt_ref, xdtt_ref, expa_ref, *rest,
                             n_groups, n_heads, head_dim, d_state, layer):
    hout_ref, yt_ref = rest[-2:]
    t = pl.program_id(0)
    tb = h0_ref.shape[0]
    ntok = bm_ref.shape[0]
    hpg = n_heads // n_groups
    gw = hpg * head_dim
    if len(hout_ref.shape) == 4:
        for other in range(hout_ref.shape[0]):
            if other != layer:
                hout_ref[other] = jnp.zeros(hout_ref.shape[1:], F32)
        hout_ref = hout_ref.at[layer]

    @pl.when(t == 0)
    def _():
        yt_ref[...] = jnp.zeros(yt_ref.shape, F32)

    tok_rows = lax.broadcasted_iota(jnp.int32, (ntok, d_state), 0)
    tok_cols = lax.broadcasted_iota(jnp.int32, (d_state, ntok), 1)

    def body(i, carry):
        tok = t * tb + i
        ea = expa_ref[pl.ds(tok, 1), :]
        for g in range(n_groups):
            rs = slice(g * gw, (g + 1) * gw)
            ns = slice(g * d_state, (g + 1) * d_state)
            rb = jnp.where(tok_rows == tok, bm_ref[:, ns], 0.0).astype(BF16)
            upd = _dot(xdtt_ref[rs, :].astype(BF16), rb)
            parts = []
            for r in range(hpg):
                h = g * hpg + r
                hr = slice(h * head_dim, (h + 1) * head_dim)
                parts.append(h0_ref[i, hr, :] * ea[:, h:h + 1])
            hnew = jnp.concatenate(parts, axis=0) + upd
            hout_ref[i, rs, :] = hnew
            rc = jnp.where(tok_cols == tok, cmt_ref[ns, :], 0.0).astype(BF16)
            yt_ref[rs, :] = yt_ref[rs, :] + _dot(hnew.astype(BF16), rc)
        return carry

    lax.fori_loop(0, tb, body, 0)


def _ssd_sample_out_kernel(x_ref, yt_ref, xs_ref, z_ref, dskip_ref, ng_ref, wout_ref, o_ref, *, n_groups):
    y = yt_ref[...].T + dskip_ref[...] * xs_ref[...]
    yn = _group_rmsnorm_gate(y, z_ref[...], ng_ref[...], n_groups).astype(BF16)
    o_ref[...] = x_ref[...] + _dot(yn, wout_ref[...])


def _ssd_sample(x, buf, h0_all, hstack, norm_layer, layer, g, wz, wxbc, wdt, cw, cb, dtb, alog, dskip, ng, wout, e,
                dims):
    m, d = x.shape
    n_groups, n_heads, head_dim, d_state = dims
    d_inner = n_heads * head_dim
    gn = n_groups * d_state
    consts = (wz, wxbc, wdt, cw, cb, dtb, alog)
    full = lambda a: pl.BlockSpec(a.shape, lambda *_: (0,) * a.ndim)
    outs = (jax.ShapeDtypeStruct((m, d_inner), F32),
            jax.ShapeDtypeStruct((m, d_inner), F32),
            jax.ShapeDtypeStruct((m, gn), F32),
            jax.ShapeDtypeStruct((gn, m), F32),
            jax.ShapeDtypeStruct((d_inner, m), F32),
            jax.ShapeDtypeStruct((m, V7X_LANES), F32),
            jax.ShapeDtypeStruct(buf.shape, F32))
    z, xs, bm, cmt, xdtt, expa, nbuf = pl.pallas_call(
        functools.partial(_ssd_sample_in_kernel, d_inner=d_inner, gn=gn),
        out_shape=outs,
        grid=(1,),
        in_specs=[full(x), full(buf), _layer_spec(g, norm_layer)] + [_layer_spec(c, layer) for c in consts]
        + [_const_spec(e)],
        out_specs=tuple(pl.BlockSpec(o.shape, lambda *_, n=len(o.shape): (0,) * n) for o in outs),
        compiler_params=_cparams("arbitrary"),
        name="ssd_sample_in",
    )(x, buf, g, *consts, e)

    first = hstack is None
    tile = SAMPLE_STATE_TILE // (h0_all.shape[0] if first else 1)
    tb = tile if tile and m % tile == 0 else m
    st = pl.BlockSpec((None, tb, d_inner, d_state), lambda i: (layer, i, 0, 0))
    ins = [h0_all, bm, cmt, xdtt, expa]
    in_specs = [st, _const_spec(bm), _const_spec(cmt), _const_spec(xdtt), _const_spec(expa)]
    aliases = {}
    if first:
        st_out = pl.BlockSpec((h0_all.shape[0], tb, d_inner, d_state), lambda i: (0, i, 0, 0))
    else:
        st_out = st
        aliases = {len(ins): 0}
        ins.append(hstack)
        in_specs.append(pl.BlockSpec(memory_space=pl.ANY))
    hstack, yt = pl.pallas_call(
        functools.partial(_ssd_sample_state_kernel, n_groups=n_groups, n_heads=n_heads, head_dim=head_dim,
                          d_state=d_state, layer=layer),
        out_shape=(jax.ShapeDtypeStruct(h0_all.shape, F32), jax.ShapeDtypeStruct((d_inner, m), F32)),
        grid=(m // tb,),
        in_specs=in_specs,
        out_specs=(st_out, pl.BlockSpec((d_inner, m), lambda i: (0, 0))),
        input_output_aliases=aliases,
        compiler_params=_cparams("arbitrary"),
        name="ssd_sample_state",
    )(*ins)

    out = pl.pallas_call(
        functools.partial(_ssd_sample_out_kernel, n_groups=n_groups),
        out_shape=jax.ShapeDtypeStruct((m, d), F32),
        grid=(1,),
        in_specs=[full(x), full(yt), full(xs), full(z), _layer_spec(dskip, layer), _layer_spec(ng, layer),
                  _layer_spec(wout, layer)],
        out_specs=full(x),
        compiler_params=_cparams("arbitrary"),
        name="ssd_sample_out",
    )(x, yt, xs, z, dskip, ng, wout)
    return out, nbuf, hstack


def _rows(v):
    return v.reshape(v.shape[0], 1, v.shape[1]).astype(F32)


def _pad_lanes(v):
    return jnp.pad(v, [(0, 0)] * (v.ndim - 1) + [(0, V7X_LANES - v.shape[-1])])


def kernel(x_prompt, x_sample, state_conv, state_ssd_conv, state_ssd, p_prompt, p_sample, norm_ffn1, w_ffn1_gate, w_ffn1_up, w_ffn1_down, norm_mix, norm_ffn2, w_ffn2_gate, w_ffn2_up, w_ffn2_down, norm_ple, w_ple_gate, w_ple_proj, cm_w_in, cm_b_in, cm_dw, cm_dw_b, cm_ln_g, cm_ln_b, cm_w_out, cm_b_out, ssd_w_in, ssd_conv_w, ssd_conv_b, ssd_dt_bias, ssd_A_log, ssd_D, ssd_norm, ssd_w_out, final_norm):
    depth = norm_ffn1.shape[0]
    bp, lp, d = x_prompt.shape
    bs = x_sample.shape[0]
    n_heads = ssd_dt_bias.shape[1]
    head_dim, d_state = state_ssd.shape[3], state_ssd.shape[4]
    d_inner = n_heads * head_dim
    conv_dim = ssd_conv_w.shape[2]
    n_groups = (conv_dim - d_inner) // (2 * d_state)
    dims = (n_groups, n_heads, head_dim, d_state)
    assert n_heads <= V7X_LANES and lp % SSD_CHUNK == 0 and x_sample.shape[1] == 1

    bf = lambda w: w.astype(BF16)
    ffn1 = (_rows(norm_ffn1), bf(w_ffn1_gate), bf(w_ffn1_up), bf(w_ffn1_down))
    tail = (_rows(norm_ffn2), bf(w_ffn2_gate), bf(w_ffn2_up), bf(w_ffn2_down), _rows(norm_ple), bf(w_ple_gate),
            bf(w_ple_proj))
    fnorm = final_norm.reshape(1, d)
    nmix = _rows(norm_mix)
    cmw = (bf(cm_w_in), _rows(cm_b_in), cm_dw, _rows(cm_dw_b), _rows(cm_ln_g), _rows(cm_ln_b), bf(cm_w_out),
           _rows(cm_b_out))
    ssw = (bf(ssd_w_in[:, :, :d_inner]), bf(ssd_w_in[:, :, d_inner:d_inner + conv_dim]),
           bf(_pad_lanes(ssd_w_in[:, :, d_inner + conv_dim:])), ssd_conv_w, _rows(ssd_conv_b),
           _rows(_pad_lanes(ssd_dt_bias)), _rows(_pad_lanes(ssd_A_log)),
           _rows(jnp.repeat(ssd_D, head_dim, axis=1)), _rows(ssd_norm), bf(ssd_w_out))
    expand = (lax.broadcasted_iota(jnp.int32, (V7X_LANES, d_inner), 0)
              == lax.broadcasted_iota(jnp.int32, (V7X_LANES, d_inner), 1) // head_dim).astype(BF16)

    xp = x_prompt.reshape(bp * lp, d)
    xsm = x_sample.reshape(bs, d)
    pp = p_prompt.reshape(depth, bp * lp, -1)
    ps = p_sample.reshape(depth, bs, -1)
    h0_all = state_ssd.reshape(state_ssd.shape[0], bs, d_inner, d_state)
    conv_p, xbc_p, ssm_p, xbc_s = [], [], [], []
    conv_s = ssm_s = None

    for i in range(depth):
        j = i // 2
        xp = _ffn(xp, i, *ffn1)
        xsm = _ffn(xsm, i, *ffn1)
        if i % 2 == 0:
            xp3, nb = _conv_prompt(xp.reshape(bp, lp, d), i, j, nmix, *cmw)
            xp = xp3.reshape(bp * lp, d)
            conv_p.append(nb)
            xsm, conv_s = _conv_sample(xsm, state_conv, conv_s, i, j, nmix, *cmw)
        else:
            xp3, nb, hf = _ssd_prompt(xp.reshape(bp, lp, d), i, j, nmix, *ssw, dims)
            xp = xp3.reshape(bp * lp, d)
            xbc_p.append(nb)
            ssm_p.append(hf.reshape(bp, n_heads, head_dim, d_state))
            xsm, nb, ssm_s = _ssd_sample(xsm, state_ssd_conv[j].reshape(bs, -1), h0_all, ssm_s, i, j, nmix, *ssw,
                                         expand, dims)
            xbc_s.append(nb.reshape(state_ssd_conv.shape[1:]))
        last = i == depth - 1
        xp = _ffn_ple(xp, pp, i, *tail, fnorm, final_norm=last)
        xsm = _ffn_ple(xsm, ps, i, *tail, fnorm, final_norm=last)

    return (xp.reshape(bp, lp, d), xsm.reshape(bs, 1, d),
            jnp.stack(conv_p), jnp.stack(xbc_p), jnp.stack(ssm_p),
            conv_s, jnp.stack(xbc_s), ssm_s.reshape(state_ssd.shape))
```

```python
import functools

import jax
import jax.numpy as jnp
from jax import lax
from jax.experimental import pallas as pl
from jax.experimental.pallas import tpu as pltpu

F32 = jnp.float32
BF16 = jnp.bfloat16
EPS = 1e-6

V7X_LANES = 128
V7X_SUBLANES = 8
V7X_VMEM_LIMIT_BYTES = 56 * 1024 * 1024

SSD_CHUNK = 128
SSD_TILE = 512
TOKEN_TILE = 1024
CONV_TILE = 1024
CONV_ROWS = 128
SAMPLE_CONV_TILE = 32
SAMPLE_STATE_TILE = 8


def _cparams(*sem):
    return pltpu.CompilerParams(dimension_semantics=sem, vmem_limit_bytes=V7X_VMEM_LIMIT_BYTES)


def _const_spec(arr):
    nd = arr.ndim
    return pl.BlockSpec(arr.shape, lambda *_: (0,) * nd, pipeline_mode=pl.Buffered(1))


def _layer_spec(arr, layer):
    nd = arr.ndim
    return pl.BlockSpec((None,) + arr.shape[1:], lambda *_: (layer,) + (0,) * (nd - 1),
                        pipeline_mode=pl.Buffered(1))


def _dot(a, b):
    return jnp.dot(a, b, preferred_element_type=F32)


def _rmsnorm(x, g):
    return x * lax.rsqrt(jnp.mean(x * x, axis=-1, keepdims=True) + EPS) * g


def _layernorm(x, g, b):
    mu = jnp.mean(x, axis=-1, keepdims=True)
    xc = x - mu
    var = jnp.mean(xc * xc, axis=-1, keepdims=True)
    return xc * lax.rsqrt(var + EPS) * g + b


def _sigmoid(x):
    return 0.5 * jnp.tanh(0.5 * x) + 0.5


def _silu(x):
    h = 0.5 * x
    return h * jnp.tanh(h) + h


def _softplus(x):
    return jnp.maximum(x, 0.0) + jnp.log(1.0 + jnp.exp(-jnp.abs(x)))


def _swiglu(x, g, wg_ref, wu_ref, wd_ref):
    h = _rmsnorm(x, g).astype(BF16)
    gate = _dot(h, wg_ref[...])
    up = _dot(h, wu_ref[...])
    act = (_silu(gate) * up).astype(BF16)
    return _dot(act, wd_ref[...])


def _ffn_kernel(x_ref, g_ref, wg_ref, wu_ref, wd_ref, o_ref):
    x = x_ref[...]
    o_ref[...] = x + 0.5 * _swiglu(x, g_ref[...], wg_ref, wu_ref, wd_ref)


def _ffn_ple_kernel(x_ref, p_ref, g_ref, wg_ref, wu_ref, wd_ref, gp_ref, wpg_ref, wpp_ref, fn_ref, o_ref,
                    *, final_norm):
    x = x_ref[...]
    x = x + 0.5 * _swiglu(x, g_ref[...], wg_ref, wu_ref, wd_ref)
    h = _rmsnorm(x, gp_ref[...]).astype(BF16)
    gate = _sigmoid(_dot(h, wpg_ref[...]))
    x = x + gate * _dot(p_ref[...].astype(BF16), wpp_ref[...])
    if final_norm:
        x = _rmsnorm(x, fn_ref[...])
    o_ref[...] = x


def _token_tile(m):
    return TOKEN_TILE if m % TOKEN_TILE == 0 else m


def _ffn(x, layer, g, wg, wu, wd):
    m, d = x.shape
    tm = _token_tile(m)
    row = pl.BlockSpec((tm, d), lambda i: (i, 0))
    consts = (g, wg, wu, wd)
    return pl.pallas_call(
        _ffn_kernel,
        out_shape=jax.ShapeDtypeStruct((m, d), F32),
        grid=(m // tm,),
        in_specs=[row] + [_layer_spec(c, layer) for c in consts],
        out_specs=row,
        compiler_params=_cparams("parallel"),
        name="ffn",
    )(x, *consts)


def _ffn_ple(x, p, layer, g, wg, wu, wd, gp, wpg, wpp, fn, final_norm):
    m, d = x.shape
    tm = _token_tile(m)
    row = pl.BlockSpec((tm, d), lambda i: (i, 0))
    prow = pl.BlockSpec((None, tm, p.shape[2]), lambda i: (layer, i, 0))
    consts = (g, wg, wu, wd, gp, wpg, wpp)
    return pl.pallas_call(
        functools.partial(_ffn_ple_kernel, final_norm=final_norm),
        out_shape=jax.ShapeDtypeStruct((m, d), F32),
        grid=(m // tm,),
        in_specs=[row, prow] + [_layer_spec(c, layer) for c in consts] + [_const_spec(fn)],
        out_specs=row,
        compiler_params=_cparams("parallel"),
        name="ffn_ple",
    )(x, p, *consts, fn)


def _glu_in(x, g, win_ref, bin_ref):
    d = x.shape[-1]
    u = _dot(_rmsnorm(x, g).astype(BF16), win_ref[...]) + bin_ref[...]
    return u[:, :d] * _sigmoid(u[:, d:])


def _conv_tail(x, v, lng, lnb, wout_ref, bout_ref):
    v = _silu(_layernorm(v, lng, lnb)).astype(BF16)
    return x + _dot(v, wout_ref[...]) + bout_ref[...]


def _conv_prompt_kernel(x_ref, g_ref, win_ref, bin_ref, dw_ref, dwb_ref, lng_ref, lnb_ref, wout_ref, bout_ref,
                        o_ref, nbuf_ref, slab_ref, v_ref, *, taps, row_block):
    t = pl.program_id(1)
    tl = x_ref.shape[0]
    nslab = slab_ref.shape[0]
    pad = slab_ref.shape[1] - tl
    hist = taps - 1

    @pl.when(t == 0)
    def _():
        slab_ref[:, 0:pad, :] = jnp.zeros((nslab, pad, V7X_LANES), F32)

    x = x_ref[...]
    glu = _glu_in(x, g_ref[...], win_ref, bin_ref)
    for j in range(nslab):
        slab_ref[j, pad:pad + tl, :] = glu[:, j * V7X_LANES:(j + 1) * V7X_LANES]

    for j in range(nslab):
        ls = slice(j * V7X_LANES, (j + 1) * V7X_LANES)
        for r0 in range(0, tl, row_block):
            acc = jnp.broadcast_to(dwb_ref[:, ls], (row_block, V7X_LANES))
            for k in range(taps):
                lo = r0 + pad - hist + k
                acc = acc + dw_ref[k:k + 1, ls] * slab_ref[j, lo:lo + row_block, :]
            v_ref[r0:r0 + row_block, ls] = acc
    o_ref[...] = _conv_tail(x, v_ref[...], lng_ref[...], lnb_ref[...], wout_ref, bout_ref)

    @pl.when(t == pl.num_programs(1) - 1)
    def _():
        for j in range(nslab):
            nbuf_ref[:, j * V7X_LANES:(j + 1) * V7X_LANES] = slab_ref[j, pad + tl - hist:pad + tl, :]

    slab_ref[:, 0:pad, :] = slab_ref[:, tl:tl + pad, :]


def _conv_prompt(x, norm_layer, layer, g, win, b_in, dw, dwb, lng, lnb, wout, bout):
    b, l, d = x.shape
    taps = dw.shape[1]
    tl = CONV_TILE if l % CONV_TILE == 0 else l
    pad = -(-(taps - 1) // V7X_SUBLANES) * V7X_SUBLANES
    row = pl.BlockSpec((None, tl, d), lambda i, j: (i, j, 0))
    consts = (win, b_in, dw, dwb, lng, lnb, wout, bout)
    return pl.pallas_call(
        functools.partial(_conv_prompt_kernel, taps=taps, row_block=min(tl, CONV_ROWS)),
        out_shape=(jax.ShapeDtypeStruct((b, l, d), F32), jax.ShapeDtypeStruct((b, taps - 1, d), F32)),
        grid=(b, l // tl),
        in_specs=[row, _layer_spec(g, norm_layer)] + [_layer_spec(c, layer) for c in consts],
        out_specs=(row, pl.BlockSpec((None, taps - 1, d), lambda i, j: (i, 0, 0))),
        scratch_shapes=[pltpu.VMEM((d // V7X_LANES, tl + pad, V7X_LANES), F32), pltpu.VMEM((tl, d), F32)],
        compiler_params=_cparams("parallel", "arbitrary"),
        name="conv_prompt",
    )(x, g, *consts)


def _conv_sample_kernel(x_ref, buf_ref, g_ref, win_ref, bin_ref, dw_ref, dwb_ref, lng_ref, lnb_ref, wout_ref,
                        bout_ref, *rest, taps, layer):
    o_ref, nbuf_ref = rest[-2:]
    if len(nbuf_ref.shape) == 4:
        for other in range(nbuf_ref.shape[0]):
            if other != layer:
                nbuf_ref[other] = jnp.zeros(nbuf_ref.shape[1:], F32)
        nbuf_ref = nbuf_ref.at[layer]
    x = x_ref[...]
    hist = taps - 1
    glu = _glu_in(x, g_ref[...], win_ref, bin_ref)
    acc = dwb_ref[...] + dw_ref[hist:taps, :] * glu
    for k in range(hist):
        acc = acc + dw_ref[k:k + 1, :] * buf_ref[:, k, :]
    for k in range(hist - 1):
        nbuf_ref[:, k, :] = buf_ref[:, k + 1, :]
    nbuf_ref[:, hist - 1, :] = glu
    o_ref[...] = _conv_tail(x, acc, lng_ref[...], lnb_ref[...], wout_ref, bout_ref)


def _conv_sample(x, buf_all, stack, norm_layer, layer, g, win, b_in, dw, dwb, lng, lnb, wout, bout):
    m, d = x.shape
    taps = dw.shape[1]
    n_layers, _, hist, _ = buf_all.shape
    first = stack is None
    tile = SAMPLE_CONV_TILE // (n_layers if first else 1)
    tb = tile if tile and m % tile == 0 else m
    row = pl.BlockSpec((tb, d), lambda i: (i, 0))
    brow = pl.BlockSpec((None, tb, hist, d), lambda i: (layer, i, 0, 0))
    consts = (win, b_in, dw, dwb, lng, lnb, wout, bout)
    ins = [x, buf_all, g, *consts]
    in_specs = [row, brow, _layer_spec(g, norm_layer)] + [_layer_spec(c, layer) for c in consts]
    aliases = {}
    if first:
        srow = pl.BlockSpec((n_layers, tb, hist, d), lambda i: (0, i, 0, 0))
    else:
        srow = brow
        aliases = {len(ins): 1}
        ins.append(stack)
        in_specs.append(pl.BlockSpec(memory_space=pl.ANY))
    return pl.pallas_call(
        functools.partial(_conv_sample_kernel, taps=taps, layer=layer),
        out_shape=(jax.ShapeDtypeStruct((m, d), F32), jax.ShapeDtypeStruct(buf_all.shape, F32)),
        grid=(m // tb,),
        in_specs=in_specs,
        out_specs=(row, srow),
        input_output_aliases=aliases,
        compiler_params=_cparams("arbitrary"),
        name="conv_sample",
    )(*ins)


def _group_rmsnorm_gate(y, z, ng, n_groups):
    y = y * _silu(z)
    gw = y.shape[-1] // n_groups
    outs = []
    for g in range(n_groups):
        s = y[:, g * gw:(g + 1) * gw]
        outs.append(s * lax.rsqrt(jnp.mean(s * s, axis=-1, keepdims=True) + EPS))
    return jnp.concatenate(outs, axis=-1) * ng


def _prefix_sum_rows(a):
    n = a.shape[0]
    rows = lax.broadcasted_iota(jnp.int32, a.shape, 0)
    sh = 1
    while sh < n:
        a = a + jnp.where(rows >= sh, pltpu.roll(a, sh, 0), 0.0)
        sh *= 2
    return a


def _ssd_prompt_kernel(x_ref, g_ref, wz_ref, wxbc_ref, wdt_ref, cw_ref, cb_ref, dtb_ref, alog_ref, dskip_ref,
                       ng_ref, wout_ref, o_ref, nbuf_ref, hfin_ref, slab_ref, hn_ref, z_ref, dt_ref, y_ref, yn_ref,
                       ht_ref, *, n_groups, n_heads, head_dim, d_state, chunk, col_block, row_block):
    t = pl.program_id(1)
    tl = x_ref.shape[0]
    nslab = slab_ref.shape[0]
    pad = slab_ref.shape[1] - tl
    d_inner = n_heads * head_dim
    hpg = n_heads // n_groups
    gw = hpg * head_dim
    ck = cw_ref.shape[0]
    xslabs = d_inner // V7X_LANES
    assert d_state == V7X_LANES and gw % V7X_LANES == 0

    @pl.when(t == 0)
    def _():
        slab_ref[:, 0:pad, :] = jnp.zeros((nslab, pad, V7X_LANES), F32)
        ht_ref[...] = jnp.zeros(ht_ref.shape, F32)

    for r0 in range(0, tl, row_block):
        hn_ref[r0:r0 + row_block, :] = _rmsnorm(x_ref[r0:r0 + row_block, :], g_ref[...]).astype(BF16)
    hn = hn_ref[...]
    dt_ref[...] = _softplus(_dot(hn, wdt_ref[...]) + dtb_ref[...])
    for c0 in range(0, nslab * V7X_LANES, col_block):
        pre = _dot(hn, wxbc_ref[:, c0:c0 + col_block])
        for jj in range(col_block // V7X_LANES):
            slab_ref[c0 // V7X_LANES + jj, pad:pad + tl, :] = pre[:, jj * V7X_LANES:(jj + 1) * V7X_LANES]
    z_ref[...] = _dot(hn, wz_ref[...])
    for j in range(nslab):
        ls = slice(j * V7X_LANES, (j + 1) * V7X_LANES)
        tail = slab_ref[j, tl:tl + pad, :]
        for r0 in reversed(range(0, tl, row_block)):
            acc = jnp.broadcast_to(cb_ref[:, ls], (row_block, V7X_LANES))
            for k in range(ck):
                lo = r0 + pad - (ck - 1) + k
                acc = acc + cw_ref[k:k + 1, ls] * slab_ref[j, lo:lo + row_block, :]
            slab_ref[j, pad + r0:pad + r0 + row_block, :] = _silu(acc)
        slab_ref[j, 0:pad, :] = tail

    @pl.when(t == pl.num_programs(1) - 1)
    def _():
        for j in range(nslab):
            nbuf_ref[:, j * V7X_LANES:(j + 1) * V7X_LANES] = slab_ref[j, pad - (ck - 1):pad, :]

    neg_a = jnp.exp(alog_ref[...])
    hps = V7X_LANES // head_dim
    lane = lax.broadcasted_iota(jnp.int32, (1, V7X_LANES), 1)
    causal = lax.broadcasted_iota(jnp.int32, (chunk, chunk), 0) >= lax.broadcasted_iota(jnp.int32, (chunk, chunk), 1)

    def chunk_body(c, carry):
        r0 = pl.multiple_of(c * chunk, chunk)
        rows = pl.ds(pl.multiple_of(pad + c * chunk, V7X_SUBLANES), chunk)
        dt = dt_ref[pl.ds(r0, chunk), :]
        acum = _prefix_sum_rows(-dt * neg_a)
        a_last = acum[chunk - 1:chunk, :]
        acum_t = acum.T
        dt_t = dt.T
        decdt_t = (dt * jnp.exp(a_last - acum)).T
        cdec = jnp.exp(a_last)
        for g in range(n_groups):
            bg = slab_ref[xslabs + g, rows, :]
            cg = slab_ref[xslabs + n_groups + g, rows, :]
            cbm = lax.dot_general(cg.astype(BF16), bg.astype(BF16), (((1,), (1,)), ((), ())),
                                  preferred_element_type=F32)
            bg_t = bg.T
            for sl in range(g * gw // V7X_LANES, (g + 1) * gw // V7X_LANES):
                ls = slice(sl * V7X_LANES, (sl + 1) * V7X_LANES)
                xs_sl = slab_ref[sl, rows, :]
                hprev = ht_ref[:, ls]
                y_sl = upd_sl = cdec_sl = None
                for u in range(hps):
                    h = sl * hps + u
                    mine = (lane >= u * head_dim) & (lane < (u + 1) * head_dim)
                    xm = jnp.where(mine, xs_sl, 0.0).astype(BF16)
                    hm = jnp.where(mine, hprev, 0.0).astype(BF16)
                    colb = jnp.broadcast_to(acum[:, h:h + 1], (chunk, chunk))
                    lmat = jnp.exp(jnp.where(causal, colb - acum_t[h:h + 1, :], -jnp.inf))
                    lhs = jnp.concatenate([cbm * lmat * dt_t[h:h + 1, :], cg * jnp.exp(colb)], axis=1)
                    yh = _dot(lhs.astype(BF16), jnp.concatenate([xm, hm], axis=0))
                    uh = _dot((bg_t * decdt_t[h:h + 1, :]).astype(BF16), xm)
                    cd = jnp.where(mine, cdec[:, h:h + 1], 0.0)
                    y_sl = yh if u == 0 else y_sl + yh
                    upd_sl = uh if u == 0 else upd_sl + uh
                    cdec_sl = cd if u == 0 else cdec_sl + cd
                y_ref[pl.ds(r0, chunk), ls] = y_sl
                ht_ref[:, ls] = hprev * cdec_sl + upd_sl
        return carry

    lax.fori_loop(0, tl // chunk, chunk_body, 0)

    out = x_ref[...]
    for g in range(n_groups):
        gs = slice(g * gw, (g + 1) * gw)
        for r0 in range(0, tl, row_block):
            rs = slice(r0, r0 + row_block)
            xs = jnp.concatenate([slab_ref[j, pad + r0:pad + r0 + row_block, :]
                                  for j in range(g * gw // V7X_LANES, (g + 1) * gw // V7X_LANES)], axis=1)
            yg = (y_ref[rs, gs] + dskip_ref[:, gs] * xs) * _silu(z_ref[rs, gs])
            yg = yg * lax.rsqrt(jnp.mean(yg * yg, axis=-1, keepdims=True) + EPS) * ng_ref[:, gs]
            yn_ref[rs, gs] = yg.astype(BF16)
        out = out + _dot(yn_ref[:, gs], wout_ref[gs, :])
    o_ref[...] = out

    @pl.when(t == pl.num_programs(1) - 1)
    def _():
        hfin_ref[...] = ht_ref[...].T


def _ssd_prompt(x, norm_layer, layer, g, wz, wxbc, wdt, cw, cb, dtb, alog, dskip, ng, wout, dims):
    b, l, d = x.shape
    n_groups, n_heads, head_dim, d_state = dims
    d_inner = n_heads * head_dim
    cd = wxbc.shape[2]
    ck = cw.shape[1]
    tl = SSD_TILE if l % SSD_TILE == 0 else l
    q = SSD_CHUNK if tl % SSD_CHUNK == 0 else tl
    row = pl.BlockSpec((None, tl, d), lambda i, j: (i, j, 0))
    consts = (wz, wxbc, wdt, cw, cb, dtb, alog, dskip, ng, wout)
    return pl.pallas_call(
        functools.partial(_ssd_prompt_kernel, n_groups=n_groups, n_heads=n_heads, head_dim=head_dim,
                          d_state=d_state, chunk=q, col_block=min(cd, 2 * V7X_LANES), row_block=q),
        out_shape=(jax.ShapeDtypeStruct((b, l, d), F32),
                   jax.ShapeDtypeStruct((b, ck - 1, cd), F32),
                   jax.ShapeDtypeStruct((b, d_inner, d_state), F32)),
        grid=(b, l // tl),
        in_specs=[row, _layer_spec(g, norm_layer)] + [_layer_spec(c, layer) for c in consts],
        out_specs=(row,
                   pl.BlockSpec((None, ck - 1, cd), lambda i, j: (i, 0, 0)),
                   pl.BlockSpec((None, d_inner, d_state), lambda i, j: (i, 0, 0))),
        scratch_shapes=[pltpu.VMEM((cd // V7X_LANES, tl + V7X_SUBLANES, V7X_LANES), F32),
                        pltpu.VMEM((tl, d), BF16),
                        pltpu.VMEM((tl, d_inner), F32),
                        pltpu.VMEM((tl, V7X_LANES), F32),
                        pltpu.VMEM((tl, d_inner), F32),
                        pltpu.VMEM((tl, d_inner), BF16),
                        pltpu.VMEM((d_state, d_inner), F32)],
        compiler_params=_cparams("parallel", "arbitrary"),
        name="ssd_prompt",
    )(x, g, *consts)


def _expand_heads(v, e_ref):
    hi = v.astype(BF16)
    r1 = v - hi.astype(F32)
    mid = r1.astype(BF16)
    lo = (r1 - mid.astype(F32)).astype(BF16)
    e = e_ref[...]
    return _dot(hi, e) + _dot(mid, e) + _dot(lo, e)


def _ssd_sample_in_kernel(x_ref, buf_ref, g_ref, wz_ref, wxbc_ref, wdt_ref, cw_ref, cb_ref, dtb_ref, alog_ref,
                          e_ref, z_ref, xs_ref, bm_ref, cmt_ref, xdtt_ref, expa_ref, nbuf_ref,
                          *, d_inner, gn):
    x = x_ref[...]
    hn = _rmsnorm(x, g_ref[...]).astype(BF16)
    z_ref[...] = _dot(hn, wz_ref[...])
    new = _dot(hn, wxbc_ref[...])
    cd = new.shape[1]
    ck = cw_ref.shape[0]
    xbc = cb_ref[...] + cw_ref[ck - 1:ck, :] * new
    for k in range(ck - 1):
        xbc = xbc + cw_ref[k:k + 1, :] * buf_ref[:, k * cd:(k + 1) * cd]
    nbuf_ref[:, 0:(ck - 2) * cd] = buf_ref[:, cd:(ck - 1) * cd]
    nbuf_ref[:, (ck - 2) * cd:(ck - 1) * cd] = new
    xbc = _silu(xbc)
    xs = xbc[:, :d_inner]
    dt = _softplus(_dot(hn, wdt_ref[...]) + dtb_ref[...])
    xs_ref[...] = xs
    bm_ref[...] = xbc[:, d_inner:d_inner + gn]
    cmt_ref[...] = xbc[:, d_inner + gn:].T
    xdtt_ref[...] = (xs * _expand_heads(dt, e_ref)).T
    expa_ref[...] = jnp.exp(dt * (-jnp.exp(alog_ref[...])))


def _ssd_sample_state_kernel(h0_ref, bm_ref, cmt_ref, xdtt_ref, expa_ref, *rest,
                             n_groups, n_heads, head_dim, d_state, layer):
    hout_ref, yt_ref = rest[-2:]
    t = pl.program_id(0)
    tb = h0_ref.shape[0]
    ntok = bm_ref.shape[0]
    hpg = n_heads // n_groups
    gw = hpg * head_dim
    if len(hout_ref.shape) == 4:
        for other in range(hout_ref.shape[0]):
            if other != layer:
                hout_ref[other] = jnp.zeros(hout_ref.shape[1:], F32)
        hout_ref = hout_ref.at[layer]

    @pl.when(t == 0)
    def _():
        yt_ref[...] = jnp.zeros(yt_ref.shape, F32)

    tok_rows = lax.broadcasted_iota(jnp.int32, (ntok, d_state), 0)
    tok_cols = lax.broadcasted_iota(jnp.int32, (d_state, ntok), 1)

    def body(i, carry):
        tok = t * tb + i
        ea = expa_ref[pl.ds(tok, 1), :]
        for g in range(n_groups):
            rs = slice(g * gw, (g + 1) * gw)
            ns = slice(g * d_state, (g + 1) * d_state)
            rb = jnp.where(tok_rows == tok, bm_ref[:, ns], 0.0).astype(BF16)
            upd = _dot(xdtt_ref[rs, :].astype(BF16), rb)
            parts = []
            for r in range(hpg):
                h = g * hpg + r
                hr = slice(h * head_dim, (h + 1) * head_dim)
                parts.append(h0_ref[i, hr, :] * ea[:, h:h + 1])
            hnew = jnp.concatenate(parts, axis=0) + upd
            hout_ref[i, rs, :] = hnew
            rc = jnp.where(tok_cols == tok, cmt_ref[ns, :], 0.0).astype(BF16)
            yt_ref[rs, :] = yt_ref[rs, :] + _dot(hnew.astype(BF16), rc)
        return carry

    lax.fori_loop(0, tb, body, 0)


def _ssd_sample_out_kernel(x_ref, yt_ref, xs_ref, z_ref, dskip_ref, ng_ref, wout_ref, o_ref, *, n_groups):
    y = yt_ref[...].T + dskip_ref[...] * xs_ref[...]
    yn = _group_rmsnorm_gate(y, z_ref[...], ng_ref[...], n_groups).astype(BF16)
    o_ref[...] = x_ref[...] + _dot(yn, wout_ref[...])


def _ssd_sample(x, buf, h0_all, hstack, norm_layer, layer, g, wz, wxbc, wdt, cw, cb, dtb, alog, dskip, ng, wout, e,
                dims):
    m, d = x.shape
    n_groups, n_heads, head_dim, d_state = dims
    d_inner = n_heads * head_dim
    gn = n_groups * d_state
    consts = (wz, wxbc, wdt, cw, cb, dtb, alog)
    full = lambda a: pl.BlockSpec(a.shape, lambda *_: (0,) * a.ndim)
    outs = (jax.ShapeDtypeStruct((m, d_inner), F32),
            jax.ShapeDtypeStruct((m, d_inner), F32),
            jax.ShapeDtypeStruct((m, gn), F32),
            jax.ShapeDtypeStruct((gn, m), F32),
            jax.ShapeDtypeStruct((d_inner, m), F32),
            jax.ShapeDtypeStruct((m, V7X_LANES), F32),
            jax.ShapeDtypeStruct(buf.shape, F32))
    z, xs, bm, cmt, xdtt, expa, nbuf = pl.pallas_call(
        functools.partial(_ssd_sample_in_kernel, d_inner=d_inner, gn=gn),
        out_shape=outs,
        grid=(1,),
        in_specs=[full(x), full(buf), _layer_spec(g, norm_layer)] + [_layer_spec(c, layer) for c in consts]
        + [_const_spec(e)],
        out_specs=tuple(pl.BlockSpec(o.shape, lambda *_, n=len(o.shape): (0,) * n) for o in outs),
        compiler_params=_cparams("arbitrary"),
        name="ssd_sample_in",
    )(x, buf, g, *consts, e)

    first = hstack is None
    tile = SAMPLE_STATE_TILE // (h0_all.shape[0] if first else 1)
    tb = tile if tile and m % tile == 0 else m
    st = pl.BlockSpec((None, tb, d_inner, d_state), lambda i: (layer, i, 0, 0))
    ins = [h0_all, bm, cmt, xdtt, expa]
    in_specs = [st, _const_spec(bm), _const_spec(cmt), _const_spec(xdtt), _const_spec(expa)]
    aliases = {}
    if first:
        st_out = pl.BlockSpec((h0_all.shape[0], tb, d_inner, d_state), lambda i: (0, i, 0, 0))
    else:
        st_out = st
        aliases = {len(ins): 0}
        ins.append(hstack)
        in_specs.append(pl.BlockSpec(memory_space=pl.ANY))
    hstack, yt = pl.pallas_call(
        functools.partial(_ssd_sample_state_kernel, n_groups=n_groups, n_heads=n_heads, head_dim=head_dim,
                          d_state=d_state, layer=layer),
        out_shape=(jax.ShapeDtypeStruct(h0_all.shape, F32), jax.ShapeDtypeStruct((d_inner, m), F32)),
        grid=(m // tb,),
        in_specs=in_specs,
        out_specs=(st_out, pl.BlockSpec((d_inner, m), lambda i: (0, 0))),
        input_output_aliases=aliases,
        compiler_params=_cparams("arbitrary"),
        name="ssd_sample_state",
    )(*ins)

    out = pl.pallas_call(
        functools.partial(_ssd_sample_out_kernel, n_groups=n_groups),
        out_shape=jax.ShapeDtypeStruct((m, d), F32),
        grid=(1,),
        in_specs=[full(x), full(yt), full(xs), full(z), _layer_spec(dskip, layer), _layer_spec(ng, layer),
                  _layer_spec(wout, layer)],
        out_specs=full(x),
        compiler_params=_cparams("arbitrary"),
        name="ssd_sample_out",
    )(x, yt, xs, z, dskip, ng, wout)
    return out, nbuf, hstack


def _rows(v):
    return v.reshape(v.shape[0], 1, v.shape[1]).astype(F32)


def _pad_lanes(v):
    return jnp.pad(v, [(0, 0)] * (v.ndim - 1) + [(0, V7X_LANES - v.shape[-1])])


def kernel(x_prompt, x_sample, state_conv, state_ssd_conv, state_ssd, p_prompt, p_sample, norm_ffn1, w_ffn1_gate, w_ffn1_up, w_ffn1_down, norm_mix, norm_ffn2, w_ffn2_gate, w_ffn2_up, w_ffn2_down, norm_ple, w_ple_gate, w_ple_proj, cm_w_in, cm_b_in, cm_dw, cm_dw_b, cm_ln_g, cm_ln_b, cm_w_out, cm_b_out, ssd_w_in, ssd_conv_w, ssd_conv_b, ssd_dt_bias, ssd_A_log, ssd_D, ssd_norm, ssd_w_out, final_norm):
    depth = norm_ffn1.shape[0]
    bp, lp, d = x_prompt.shape
    bs = x_sample.shape[0]
    n_heads = ssd_dt_bias.shape[1]
    head_dim, d_state = state_ssd.shape[3], state_ssd.shape[4]
    d_inner = n_heads * head_dim
    conv_dim = ssd_conv_w.shape[2]
    n_groups = (conv_dim - d_inner) // (2 * d_state)
    dims = (n_groups, n_heads, head_dim, d_state)
    assert n_heads <= V7X_LANES and lp % SSD_CHUNK == 0 and x_sample.shape[1] == 1

    bf = lambda w: w.astype(BF16)
    ffn1 = (_rows(norm_ffn1), bf(w_ffn1_gate), bf(w_ffn1_up), bf(w_ffn1_down))
    tail = (_rows(norm_ffn2), bf(w_ffn2_gate), bf(w_ffn2_up), bf(w_ffn2_down), _rows(norm_ple), bf(w_ple_gate),
            bf(w_ple_proj))
    fnorm = final_norm.reshape(1, d)
    nmix = _rows(norm_mix)
    cmw = (bf(cm_w_in), _rows(cm_b_in), cm_dw, _rows(cm_dw_b), _rows(cm_ln_g), _rows(cm_ln_b), bf(cm_w_out),
           _rows(cm_b_out))
    ssw = (bf(ssd_w_in[:, :, :d_inner]), bf(ssd_w_in[:, :, d_inner:d_inner + conv_dim]),
           bf(_pad_lanes(ssd_w_in[:, :, d_inner + conv_dim:])), ssd_conv_w, _rows(ssd_conv_b),
           _rows(_pad_lanes(ssd_dt_bias)), _rows(_pad_lanes(ssd_A_log)),
           _rows(jnp.repeat(ssd_D, head_dim, axis=1)), _rows(ssd_norm), bf(ssd_w_out))
    expand = (lax.broadcasted_iota(jnp.int32, (V7X_LANES, d_inner), 0)
              == lax.broadcasted_iota(jnp.int32, (V7X_LANES, d_inner), 1) // head_dim).astype(BF16)

    xp = x_prompt.reshape(bp * lp, d)
    xsm = x_sample.reshape(bs, d)
    pp = p_prompt.reshape(depth, bp * lp, -1)
    ps = p_sample.reshape(depth, bs, -1)
    h0_all = state_ssd.reshape(state_ssd.shape[0], bs, d_inner, d_state)
    conv_p, xbc_p, ssm_p, xbc_s = [], [], [], []
    conv_s = ssm_s = None

    for i in range(depth):
        j = i // 2
        xp = _ffn(xp, i, *ffn1)
        xsm = _ffn(xsm, i, *ffn1)
        if i % 2 == 0:
            xp3, nb = _conv_prompt(xp.reshape(bp, lp, d), i, j, nmix, *cmw)
            xp = xp3.reshape(bp * lp, d)
            conv_p.append(nb)
            xsm, conv_s = _conv_sample(xsm, state_conv, conv_s, i, j, nmix, *cmw)
        else:
            xp3, nb, hf = _ssd_prompt(xp.reshape(bp, lp, d), i, j, nmix, *ssw, dims)
            xp = xp3.reshape(bp * lp, d)
            xbc_p.append(nb)
            ssm_p.append(hf.reshape(bp, n_heads, head_dim, d_state))
            xsm, nb, ssm_s = _ssd_sample(xsm, state_ssd_conv[j].reshape(bs, -1), h0_all, ssm_s, i, j, nmix, *ssw,
                                         expand, dims)
            xbc_s.append(nb.reshape(state_ssd_conv.shape[1:]))
        last = i == depth - 1
        xp = _ffn_ple(xp, pp, i, *tail, fnorm, final_norm=last)
        xsm = _ffn_ple(xsm, ps, i, *tail, fnorm, final_norm=last)

    return (xp.reshape(bp, lp, d), xsm.reshape(bs, 1, d),
            jnp.stack(conv_p), jnp.stack(xbc_p), jnp.stack(ssm_p),
            conv_s, jnp.stack(xbc_s), ssm_s.reshape(state_ssd.shape))
```

```python
import functools

import jax
import jax.numpy as jnp
from jax import lax
from jax.experimental import pallas as pl
from jax.experimental.pallas import tpu as pltpu

F32 = jnp.float32
BF16 = jnp.bfloat16
EPS = 1e-6

V7X_LANES = 128
V7X_SUBLANES = 8
V7X_VMEM_LIMIT_BYTES = 56 * 1024 * 1024

SSD_CHUNK = 128
SSD_TILE = 512
TOKEN_TILE = 1024
CONV_TILE = 1024
CONV_ROWS = 128
SAMPLE_CONV_TILE = 32
SAMPLE_STATE_TILE = 8


def _cparams(*sem):
    return pltpu.CompilerParams(dimension_semantics=sem, vmem_limit_bytes=V7X_VMEM_LIMIT_BYTES)


def _const_spec(arr):
    nd = arr.ndim
    return pl.BlockSpec(arr.shape, lambda *_: (0,) * nd, pipeline_mode=pl.Buffered(1))


def _layer_spec(arr, layer):
    nd = arr.ndim
    return pl.BlockSpec((None,) + arr.shape[1:], lambda *_: (layer,) + (0,) * (nd - 1),
                        pipeline_mode=pl.Buffered(1))


def _dot(a, b):
    return jnp.dot(a, b, preferred_element_type=F32)


def _rmsnorm(x, g):
    return x * lax.rsqrt(jnp.mean(x * x, axis=-1, keepdims=True) + EPS) * g


def _layernorm(x, g, b):
    mu = jnp.mean(x, axis=-1, keepdims=True)
    xc = x - mu
    var = jnp.mean(xc * xc, axis=-1, keepdims=True)
    return xc * lax.rsqrt(var + EPS) * g + b


def _sigmoid(x):
    return 0.5 * jnp.tanh(0.5 * x) + 0.5


def _silu(x):
    h = 0.5 * x
    return h * jnp.tanh(h) + h


def _softplus(x):
    return jnp.maximum(x, 0.0) + jnp.log(1.0 + jnp.exp(-jnp.abs(x)))


def _swiglu(x, g, wg_ref, wu_ref, wd_ref):
    h = _rmsnorm(x, g).astype(BF16)
    gate = _dot(h, wg_ref[...])
    up = _dot(h, wu_ref[...])
    act = (_silu(gate) * up).astype(BF16)
    return _dot(act, wd_ref[...])


def _ffn_kernel(x_ref, g_ref, wg_ref, wu_ref, wd_ref, o_ref):
    x = x_ref[...]
    o_ref[...] = x + 0.5 * _swiglu(x, g_ref[...], wg_ref, wu_ref, wd_ref)


def _ffn_ple_kernel(x_ref, p_ref, g_ref, wg_ref, wu_ref, wd_ref, gp_ref, wpg_ref, wpp_ref, fn_ref, o_ref,
                    *, final_norm):
    x = x_ref[...]
    x = x + 0.5 * _swiglu(x, g_ref[...], wg_ref, wu_ref, wd_ref)
    h = _rmsnorm(x, gp_ref[...]).astype(BF16)
    gate = _sigmoid(_dot(h, wpg_ref[...]))
    x = x + gate * _dot(p_ref[...].astype(BF16), wpp_ref[...])
    if final_norm:
        x = _rmsnorm(x, fn_ref[...])
    o_ref[...] = x


def _token_tile(m):
    return TOKEN_TILE if m % TOKEN_TILE == 0 else m


def _ffn(x, layer, g, wg, wu, wd):
    m, d = x.shape
    tm = _token_tile(m)
    row = pl.BlockSpec((tm, d), lambda i: (i, 0))
    consts = (g, wg, wu, wd)
    return pl.pallas_call(
        _ffn_kernel,
        out_shape=jax.ShapeDtypeStruct((m, d), F32),
        grid=(m // tm,),
        in_specs=[row] + [_layer_spec(c, layer) for c in consts],
        out_specs=row,
        compiler_params=_cparams("parallel"),
        name="ffn",
    )(x, *consts)


def _ffn_ple(x, p, layer, g, wg, wu, wd, gp, wpg, wpp, fn, final_norm):
    m, d = x.shape
    tm = _token_tile(m)
    row = pl.BlockSpec((tm, d), lambda i: (i, 0))
    prow = pl.BlockSpec((None, tm, p.shape[2]), lambda i: (layer, i, 0))
    consts = (g, wg, wu, wd, gp, wpg, wpp)
    return pl.pallas_call(
        functools.partial(_ffn_ple_kernel, final_norm=final_norm),
        out_shape=jax.ShapeDtypeStruct((m, d), F32),
        grid=(m // tm,),
        in_specs=[row, prow] + [_layer_spec(c, layer) for c in consts] + [_const_spec(fn)],
        out_specs=row,
        compiler_params=_cparams("parallel"),
        name="ffn_ple",
    )(x, p, *consts, fn)


def _glu_in(x, g, win_ref, bin_ref):
    d = x.shape[-1]
    u = _dot(_rmsnorm(x, g).astype(BF16), win_ref[...]) + bin_ref[...]
    return u[:, :d] * _sigmoid(u[:, d:])


def _conv_tail(x, v, lng, lnb, wout_ref, bout_ref):
    v = _silu(_layernorm(v, lng, lnb)).astype(BF16)
    return x + _dot(v, wout_ref[...]) + bout_ref[...]


def _conv_prompt_kernel(x_ref, g_ref, win_ref, bin_ref, dw_ref, dwb_ref, lng_ref, lnb_ref, wout_ref, bout_ref,
                        o_ref, nbuf_ref, slab_ref, v_ref, *, taps, row_block):
    t = pl.program_id(1)
    tl = x_ref.shape[0]
    nslab = slab_ref.shape[0]
    pad = slab_ref.shape[1] - tl
    hist = taps - 1

    @pl.when(t == 0)
    def _():
        slab_ref[:, 0:pad, :] = jnp.zeros((nslab, pad, V7X_LANES), F32)

    x = x_ref[...]
    glu = _glu_in(x, g_ref[...], win_ref, bin_ref)
    for j in range(nslab):
        slab_ref[j, pad:pad + tl, :] = glu[:, j * V7X_LANES:(j + 1) * V7X_LANES]

    for j in range(nslab):
        ls = slice(j * V7X_LANES, (j + 1) * V7X_LANES)
        for r0 in range(0, tl, row_block):
            acc = jnp.broadcast_to(dwb_ref[:, ls], (row_block, V7X_LANES))
            for k in range(taps):
                lo = r0 + pad - hist + k
                acc = acc + dw_ref[k:k + 1, ls] * slab_ref[j, lo:lo + row_block, :]
            v_ref[r0:r0 + row_block, ls] = acc
    o_ref[...] = _conv_tail(x, v_ref[...], lng_ref[...], lnb_ref[...], wout_ref, bout_ref)

    @pl.when(t == pl.num_programs(1) - 1)
    def _():
        for j in range(nslab):
            nbuf_ref[:, j * V7X_LANES:(j + 1) * V7X_LANES] = slab_ref[j, pad + tl - hist:pad + tl, :]

    slab_ref[:, 0:pad, :] = slab_ref[:, tl:tl + pad, :]


def _conv_prompt(x, norm_layer, layer, g, win, b_in, dw, dwb, lng, lnb, wout, bout):
    b, l, d = x.shape
    taps = dw.shape[1]
    tl = CONV_TILE if l % CONV_TILE == 0 else l
    pad = -(-(taps - 1) // V7X_SUBLANES) * V7X_SUBLANES
    row = pl.BlockSpec((None, tl, d), lambda i, j: (i, j, 0))
    consts = (win, b_in, dw, dwb, lng, lnb, wout, bout)
    return pl.pallas_call(
        functools.partial(_conv_prompt_kernel, taps=taps, row_block=min(tl, CONV_ROWS)),
        out_shape=(jax.ShapeDtypeStruct((b, l, d), F32), jax.ShapeDtypeStruct((b, taps - 1, d), F32)),
        grid=(b, l // tl),
        in_specs=[row, _layer_spec(g, norm_layer)] + [_layer_spec(c, layer) for c in consts],
        out_specs=(row, pl.BlockSpec((None, taps - 1, d), lambda i, j: (i, 0, 0))),
        scratch_shapes=[pltpu.VMEM((d // V7X_LANES, tl + pad, V7X_LANES), F32), pltpu.VMEM((tl, d), F32)],
        compiler_params=_cparams("parallel", "arbitrary"),
        name="conv_prompt",
    )(x, g, *consts)


def _conv_sample_kernel(x_ref, buf_ref, g_ref, win_ref, bin_ref, dw_ref, dwb_ref, lng_ref, lnb_ref, wout_ref,
                        bout_ref, *rest, taps, layer):
    o_ref, nbuf_ref = rest[-2:]
    if len(nbuf_ref.shape) == 4:
        for other in range(nbuf_ref.shape[0]):
            if other != layer:
                nbuf_ref[other] = jnp.zeros(nbuf_ref.shape[1:], F32)
        nbuf_ref = nbuf_ref.at[layer]
    x = x_ref[...]
    hist = taps - 1
    glu = _glu_in(x, g_ref[...], win_ref, bin_ref)
    acc = dwb_ref[...] + dw_ref[hist:taps, :] * glu
    for k in range(hist):
        acc = acc + dw_ref[k:k + 1, :] * buf_ref[:, k, :]
    for k in range(hist - 1):
        nbuf_ref[:, k, :] = buf_ref[:, k + 1, :]
    nbuf_ref[:, hist - 1, :] = glu
    o_ref[...] = _conv_tail(x, acc, lng_ref[...], lnb_ref[...], wout_ref, bout_ref)


def _conv_sample(x, buf_all, stack, norm_layer, layer, g, win, b_in, dw, dwb, lng, lnb, wout, bout):
    m, d = x.shape
    taps = dw.shape[1]
    n_layers, _, hist, _ = buf_all.shape
    first = stack is None
    tile = SAMPLE_CONV_TILE // (n_layers if first else 1)
    tb = tile if tile and m % tile == 0 else m
    row = pl.BlockSpec((tb, d), lambda i: (i, 0))
    brow = pl.BlockSpec((None, tb, hist, d), lambda i: (layer, i, 0, 0))
    consts = (win, b_in, dw, dwb, lng, lnb, wout, bout)
    ins = [x, buf_all, g, *consts]
    in_specs = [row, brow, _layer_spec(g, norm_layer)] + [_layer_spec(c, layer) for c in consts]
    aliases = {}
    if first:
        srow = pl.BlockSpec((n_layers, tb, hist, d), lambda i: (0, i, 0, 0))
    else:
        srow = brow
        aliases = {len(ins): 1}
        ins.append(stack)
        in_specs.append(pl.BlockSpec(memory_space=pl.ANY))
    return pl.pallas_call(
        functools.partial(_conv_sample_kernel, taps=taps, layer=layer),
        out_shape=(jax.ShapeDtypeStruct((m, d), F32), jax.ShapeDtypeStruct(buf_all.shape, F32)),
        grid=(m // tb,),
        in_specs=in_specs,
        out_specs=(row, srow),
        input_output_aliases=aliases,
        compiler_params=_cparams("arbitrary"),
        name="conv_sample",
    )(*ins)


def _group_rmsnorm_gate(y, z, ng, n_groups):
    y = y * _silu(z)
    gw = y.shape[-1] // n_groups
    outs = []
    for g in range(n_groups):
        s = y[:, g * gw:(g + 1) * gw]
        outs.append(s * lax.rsqrt(jnp.mean(s * s, axis=-1, keepdims=True) + EPS))
    return jnp.concatenate(outs, axis=-1) * ng


def _prefix_sum_rows(a):
    n = a.shape[0]
    rows = lax.broadcasted_iota(jnp.int32, a.shape, 0)
    sh = 1
    while sh < n:
        a = a + jnp.where(rows >= sh, pltpu.roll(a, sh, 0), 0.0)
        sh *= 2
    return a


def _ssd_prompt_kernel(x_ref, g_ref, wz_ref, wxbc_ref, wdt_ref, cw_ref, cb_ref, dtb_ref, alog_ref, dskip_ref,
                       ng_ref, wout_ref, o_ref, nbuf_ref, hfin_ref, slab_ref, hn_ref, z_ref, dt_ref, y_ref, yn_ref,
                       ht_ref, *, n_groups, n_heads, head_dim, d_state, chunk, col_block, row_block):
    t = pl.program_id(1)
    tl = x_ref.shape[0]
    nslab = slab_ref.shape[0]
    pad = slab_ref.shape[1] - tl
    d_inner = n_heads * head_dim
    hpg = n_heads // n_groups
    gw = hpg * head_dim
    ck = cw_ref.shape[0]
    xslabs = d_inner // V7X_LANES
    assert d_state == V7X_LANES and gw % V7X_LANES == 0

    @pl.when(t == 0)
    def _():
        slab_ref[:, 0:pad, :] = jnp.zeros((nslab, pad, V7X_LANES), F32)
        ht_ref[...] = jnp.zeros(ht_ref.shape, F32)

    for r0 in range(0, tl, row_block):
        hn_ref[r0:r0 + row_block, :] = _rmsnorm(x_ref[r0:r0 + row_block, :], g_ref[...]).astype(BF16)
    hn = hn_ref[...]
    dt_ref[...] = _softplus(_dot(hn, wdt_ref[...]) + dtb_ref[...])
    for c0 in range(0, nslab * V7X_LANES, col_block):
        pre = _dot(hn, wxbc_ref[:, c0:c0 + col_block])
        for jj in range(col_block // V7X_LANES):
            slab_ref[c0 // V7X_LANES + jj, pad:pad + tl, :] = pre[:, jj * V7X_LANES:(jj + 1) * V7X_LANES]
    z_ref[...] = _dot(hn, wz_ref[...])
    for j in range(nslab):
        ls = slice(j * V7X_LANES, (j + 1) * V7X_LANES)
        tail = slab_ref[j, tl:tl + pad, :]
        for r0 in reversed(range(0, tl, row_block)):
            acc = jnp.broadcast_to(cb_ref[:, ls], (row_block, V7X_LANES))
            for k in range(ck):
                lo = r0 + pad - (ck - 1) + k
                acc = acc + cw_ref[k:k + 1, ls] * slab_ref[j, lo:lo + row_block, :]
            slab_ref[j, pad + r0:pad + r0 + row_block, :] = _silu(acc)
        slab_ref[j, 0:pad, :] = tail

    @pl.when(t == pl.num_programs(1) - 1)
    def _():
        for j in range(nslab):
            nbuf_ref[:, j * V7X_LANES:(j + 1) * V7X_LANES] = slab_ref[j, pad - (ck - 1):pad, :]

    neg_a = jnp.exp(alog_ref[...])
    hps = V7X_LANES // head_dim
    lane = lax.broadcasted_iota(jnp.int32, (1, V7X_LANES), 1)
    causal = lax.broadcasted_iota(jnp.int32, (chunk, chunk), 0) >= lax.broadcasted_iota(jnp.int32, (chunk, chunk), 1)

    def chunk_body(c, carry):
        r0 = pl.multiple_of(c * chunk, chunk)
        rows = pl.ds(pl.multiple_of(pad + c * chunk, V7X_SUBLANES), chunk)
        dt = dt_ref[pl.ds(r0, chunk), :]
        acum = _prefix_sum_rows(-dt * neg_a)
        a_last = acum[chunk - 1:chunk, :]
        acum_t = acum.T
        dt_t = dt.T
        decdt_t = (dt * jnp.exp(a_last - acum)).T
        cdec = jnp.exp(a_last)
        for g in range(n_groups):
            bg = slab_ref[xslabs + g, rows, :]
            cg = slab_ref[xslabs + n_groups + g, rows, :]
            cbm = lax.dot_general(cg.astype(BF16), bg.astype(BF16), (((1,), (1,)), ((), ())),
                                  preferred_element_type=F32)
            bg_t = bg.T
            for sl in range(g * gw // V7X_LANES, (g + 1) * gw // V7X_LANES):
                ls = slice(sl * V7X_LANES, (sl + 1) * V7X_LANES)
                xs_sl = slab_ref[sl, rows, :]
                hprev = ht_ref[:, ls]
                y_sl = upd_sl = cdec_sl = None
                for u in range(hps):
                    h = sl * hps + u
                    mine = (lane >= u * head_dim) & (lane < (u + 1) * head_dim)
                    xm = jnp.where(mine, xs_sl, 0.0).astype(BF16)
                    hm = jnp.where(mine, hprev, 0.0).astype(BF16)
                    colb = jnp.broadcast_to(acum[:, h:h + 1], (chunk, chunk))
                    lmat = jnp.exp(jnp.where(causal, colb - acum_t[h:h + 1, :], -jnp.inf))
                    lhs = jnp.concatenate([cbm * lmat * dt_t[h:h + 1, :], cg * jnp.exp(colb)], axis=1)
                    yh = _dot(lhs.astype(BF16), jnp.concatenate([xm, hm], axis=0))
                    uh = _dot((bg_t * decdt_t[h:h + 1, :]).astype(BF16), xm)
                    cd = jnp.where(mine, cdec[:, h:h + 1], 0.0)
                    y_sl = yh if u == 0 else y_sl + yh
                    upd_sl = uh if u == 0 else upd_sl + uh
                    cdec_sl = cd if u == 0 else cdec_sl + cd
                y_ref[pl.ds(r0, chunk), ls] = y_sl
                ht_ref[:, ls] = hprev * cdec_sl + upd_sl
        return carry

    lax.fori_loop(0, tl // chunk, chunk_body, 0, unroll=2)

    out = x_ref[...]
    for g in range(n_groups):
        gs = slice(g * gw, (g + 1) * gw)
        for r0 in range(0, tl, row_block):
            rs = slice(r0, r0 + row_block)
            xs = jnp.concatenate([slab_ref[j, pad + r0:pad + r0 + row_block, :]
                                  for j in range(g * gw // V7X_LANES, (g + 1) * gw // V7X_LANES)], axis=1)
            yg = (y_ref[rs, gs] + dskip_ref[:, gs] * xs) * _silu(z_ref[rs, gs])
            yg = yg * lax.rsqrt(jnp.mean(yg * yg, axis=-1, keepdims=True) + EPS) * ng_ref[:, gs]
            yn_ref[rs, gs] = yg.astype(BF16)
        out = out + _dot(yn_ref[:, gs], wout_ref[gs, :])
    o_ref[...] = out

    @pl.when(t == pl.num_programs(1) - 1)
    def _():
        hfin_ref[...] = ht_ref[...].T


def _ssd_prompt(x, norm_layer, layer, g, wz, wxbc, wdt, cw, cb, dtb, alog, dskip, ng, wout, dims):
    b, l, d = x.shape
    n_groups, n_heads, head_dim, d_state = dims
    d_inner = n_heads * head_dim
    cd = wxbc.shape[2]
    ck = cw.shape[1]
    tl = SSD_TILE if l % SSD_TILE == 0 else l
    q = SSD_CHUNK if tl % SSD_CHUNK == 0 else tl
    row = pl.BlockSpec((None, tl, d), lambda i, j: (i, j, 0))
    consts = (wz, wxbc, wdt, cw, cb, dtb, alog, dskip, ng, wout)
    return pl.pallas_call(
        functools.partial(_ssd_prompt_kernel, n_groups=n_groups, n_heads=n_heads, head_dim=head_dim,
                          d_state=d_state, chunk=q, col_block=min(cd, 2 * V7X_LANES), row_block=q),
        out_shape=(jax.ShapeDtypeStruct((b, l, d), F32),
                   jax.ShapeDtypeStruct((b, ck - 1, cd), F32),
                   jax.ShapeDtypeStruct((b, d_inner, d_state), F32)),
        grid=(b, l // tl),
        in_specs=[row, _layer_spec(g, norm_layer)] + [_layer_spec(c, layer) for c in consts],
        out_specs=(row,
                   pl.BlockSpec((None, ck - 1, cd), lambda i, j: (i, 0, 0)),
                   pl.BlockSpec((None, d_inner, d_state), lambda i, j: (i, 0, 0))),
        scratch_shapes=[pltpu.VMEM((cd // V7X_LANES, tl + V7X_SUBLANES, V7X_LANES), F32),
                        pltpu.VMEM((tl, d), BF16),
                        pltpu.VMEM((tl, d_inner), F32),
                        pltpu.VMEM((tl, V7X_LANES), F32),
                        pltpu.VMEM((tl, d_inner), F32),
                        pltpu.VMEM((tl, d_inner), BF16),
                        pltpu.VMEM((d_state, d_inner), F32)],
        compiler_params=_cparams("parallel", "arbitrary"),
        name="ssd_prompt",
    )(x, g, *consts)


def _expand_heads(v, e_ref):
    hi = v.astype(BF16)
    r1 = v - hi.astype(F32)
    mid = r1.astype(BF16)
    lo = (r1 - mid.astype(F32)).astype(BF16)
    e = e_ref[...]
    return _dot(hi, e) + _dot(mid, e) + _dot(lo, e)


def _ssd_sample_in_kernel(x_ref, buf_ref, g_ref, wz_ref, wxbc_ref, wdt_ref, cw_ref, cb_ref, dtb_ref, alog_ref,
                          e_ref, z_ref, xs_ref, bm_ref, cmt_ref, xdtt_ref, expa_ref, nbuf_ref,
                          *, d_inner, gn):
    x = x_ref[...]
    hn = _rmsnorm(x, g_ref[...]).astype(BF16)
    z_ref[...] = _dot(hn, wz_ref[...])
    new = _dot(hn, wxbc_ref[...])
    cd = new.shape[1]
    ck = cw_ref.shape[0]
    xbc = cb_ref[...] + cw_ref[ck - 1:ck, :] * new
    for k in range(ck - 1):
        xbc = xbc + cw_ref[k:k + 1, :] * buf_ref[:, k * cd:(k + 1) * cd]
    nbuf_ref[:, 0:(ck - 2) * cd] = buf_ref[:, cd:(ck - 1) * cd]
    nbuf_ref[:, (ck - 2) * cd:(ck - 1) * cd] = new
    xbc = _silu(xbc)
    xs = xbc[:, :d_inner]
    dt = _softplus(_dot(hn, wdt_ref[...]) + dtb_ref[...])
    xs_ref[...] = xs
    bm_ref[...] = xbc[:, d_inner:d_inner + gn]
    cmt_ref[...] = xbc[:, d_inner + gn:].T
    xdtt_ref[...] = (xs * _expand_heads(dt, e_ref)).T
    expa_ref[...] = jnp.exp(dt * (-jnp.exp(alog_ref[...])))


def _ssd_sample_state_kernel(h0_ref, bm_ref, cmt_ref, xdtt_ref, expa_ref, *rest,
                             n_groups, n_heads, head_dim, d_state, layer):
    hout_ref, yt_ref = rest[-2:]
    t = pl.program_id(0)
    tb = h0_ref.shape[0]
    ntok = bm_ref.shape[0]
    hpg = n_heads // n_groups
    gw = hpg * head_dim
    if len(hout_ref.shape) == 4:
        for other in range(hout_ref.shape[0]):
            if other != layer:
                hout_ref[other] = jnp.zeros(hout_ref.shape[1:], F32)
        hout_ref = hout_ref.at[layer]

    @pl.when(t == 0)
    def _():
        yt_ref[...] = jnp.zeros(yt_ref.shape, F32)

    tok_rows = lax.broadcasted_iota(jnp.int32, (ntok, d_state), 0)
    tok_cols = lax.broadcasted_iota(jnp.int32, (d_state, ntok), 1)

    def body(i, carry):
        tok = t * tb + i
        ea = expa_ref[pl.ds(tok, 1), :]
        for g in range(n_groups):
            rs = slice(g * gw, (g + 1) * gw)
            ns = slice(g * d_state, (g + 1) * d_state)
            rb = jnp.where(tok_rows == tok, bm_ref[:, ns], 0.0).astype(BF16)
            upd = _dot(xdtt_ref[rs, :].astype(BF16), rb)
            parts = []
            for r in range(hpg):
                h = g * hpg + r
                hr = slice(h * head_dim, (h + 1) * head_dim)
                parts.append(h0_ref[i, hr, :] * ea[:, h:h + 1])
            hnew = jnp.concatenate(parts, axis=0) + upd
            hout_ref[i, rs, :] = hnew
            rc = jnp.where(tok_cols == tok, cmt_ref[ns, :], 0.0).astype(BF16)
            yt_ref[rs, :] = yt_ref[rs, :] + _dot(hnew.astype(BF16), rc)
        return carry

    lax.fori_loop(0, tb, body, 0)


def _ssd_sample_out_kernel(x_ref, yt_ref, xs_ref, z_ref, dskip_ref, ng_ref, wout_ref, o_ref, *, n_groups):
    y = yt_ref[...].T + dskip_ref[...] * xs_ref[...]
    yn = _group_rmsnorm_gate(y, z_ref[...], ng_ref[...], n_groups).astype(BF16)
    o_ref[...] = x_ref[...] + _dot(yn, wout_ref[...])


def _ssd_sample(x, buf, h0_all, hstack, norm_layer, layer, g, wz, wxbc, wdt, cw, cb, dtb, alog, dskip, ng, wout, e,
                dims):
    m, d = x.shape
    n_groups, n_heads, head_dim, d_state = dims
    d_inner = n_heads * head_dim
    gn = n_groups * d_state
    consts = (wz, wxbc, wdt, cw, cb, dtb, alog)
    full = lambda a: pl.BlockSpec(a.shape, lambda *_: (0,) * a.ndim)
    outs = (jax.ShapeDtypeStruct((m, d_inner), F32),
            jax.ShapeDtypeStruct((m, d_inner), F32),
            jax.ShapeDtypeStruct((m, gn), F32),
            jax.ShapeDtypeStruct((gn, m), F32),
            jax.ShapeDtypeStruct((d_inner, m), F32),
            jax.ShapeDtypeStruct((m, V7X_LANES), F32),
            jax.ShapeDtypeStruct(buf.shape, F32))
    z, xs, bm, cmt, xdtt, expa, nbuf = pl.pallas_call(
        functools.partial(_ssd_sample_in_kernel, d_inner=d_inner, gn=gn),
        out_shape=outs,
        grid=(1,),
        in_specs=[full(x), full(buf), _layer_spec(g, norm_layer)] + [_layer_spec(c, layer) for c in consts]
        + [_const_spec(e)],
        out_specs=tuple(pl.BlockSpec(o.shape, lambda *_, n=len(o.shape): (0,) * n) for o in outs),
        compiler_params=_cparams("arbitrary"),
        name="ssd_sample_in",
    )(x, buf, g, *consts, e)

    first = hstack is None
    tile = SAMPLE_STATE_TILE // (h0_all.shape[0] if first else 1)
    tb = tile if tile and m % tile == 0 else m
    st = pl.BlockSpec((None, tb, d_inner, d_state), lambda i: (layer, i, 0, 0))
    ins = [h0_all, bm, cmt, xdtt, expa]
    in_specs = [st, _const_spec(bm), _const_spec(cmt), _const_spec(xdtt), _const_spec(expa)]
    aliases = {}
    if first:
        st_out = pl.BlockSpec((h0_all.shape[0], tb, d_inner, d_state), lambda i: (0, i, 0, 0))
    else:
        st_out = st
        aliases = {len(ins): 0}
        ins.append(hstack)
        in_specs.append(pl.BlockSpec(memory_space=pl.ANY))
    hstack, yt = pl.pallas_call(
        functools.partial(_ssd_sample_state_kernel, n_groups=n_groups, n_heads=n_heads, head_dim=head_dim,
                          d_state=d_state, layer=layer),
        out_shape=(jax.ShapeDtypeStruct(h0_all.shape, F32), jax.ShapeDtypeStruct((d_inner, m), F32)),
        grid=(m // tb,),
        in_specs=in_specs,
        out_specs=(st_out, pl.BlockSpec((d_inner, m), lambda i: (0, 0))),
        input_output_aliases=aliases,
        compiler_params=_cparams("arbitrary"),
        name="ssd_sample_state",
    )(*ins)

    out = pl.pallas_call(
        functools.partial(_ssd_sample_out_kernel, n_groups=n_groups),
        out_shape=jax.ShapeDtypeStruct((m, d), F32),
        grid=(1,),
        in_specs=[full(x), full(yt), full(xs), full(z), _layer_spec(dskip, layer), _layer_spec(ng, layer),
                  _layer_spec(wout, layer)],
        out_specs=full(x),
        compiler_params=_cparams("arbitrary"),
        name="ssd_sample_out",
    )(x, yt, xs, z, dskip, ng, wout)
    return out, nbuf, hstack


def _rows(v):
    return v.reshape(v.shape[0], 1, v.shape[1]).astype(F32)


def _pad_lanes(v):
    return jnp.pad(v, [(0, 0)] * (v.ndim - 1) + [(0, V7X_LANES - v.shape[-1])])


def kernel(x_prompt, x_sample, state_conv, state_ssd_conv, state_ssd, p_prompt, p_sample, norm_ffn1, w_ffn1_gate, w_ffn1_up, w_ffn1_down, norm_mix, norm_ffn2, w_ffn2_gate, w_ffn2_up, w_ffn2_down, norm_ple, w_ple_gate, w_ple_proj, cm_w_in, cm_b_in, cm_dw, cm_dw_b, cm_ln_g, cm_ln_b, cm_w_out, cm_b_out, ssd_w_in, ssd_conv_w, ssd_conv_b, ssd_dt_bias, ssd_A_log, ssd_D, ssd_norm, ssd_w_out, final_norm):
    depth = norm_ffn1.shape[0]
    bp, lp, d = x_prompt.shape
    bs = x_sample.shape[0]
    n_heads = ssd_dt_bias.shape[1]
    head_dim, d_state = state_ssd.shape[3], state_ssd.shape[4]
    d_inner = n_heads * head_dim
    conv_dim = ssd_conv_w.shape[2]
    n_groups = (conv_dim - d_inner) // (2 * d_state)
    dims = (n_groups, n_heads, head_dim, d_state)
    assert n_heads <= V7X_LANES and lp % SSD_CHUNK == 0 and x_sample.shape[1] == 1

    bf = lambda w: w.astype(BF16)
    ffn1 = (_rows(norm_ffn1), bf(w_ffn1_gate), bf(w_ffn1_up), bf(w_ffn1_down))
    tail = (_rows(norm_ffn2), bf(w_ffn2_gate), bf(w_ffn2_up), bf(w_ffn2_down), _rows(norm_ple), bf(w_ple_gate),
            bf(w_ple_proj))
    fnorm = final_norm.reshape(1, d)
    nmix = _rows(norm_mix)
    cmw = (bf(cm_w_in), _rows(cm_b_in), cm_dw, _rows(cm_dw_b), _rows(cm_ln_g), _rows(cm_ln_b), bf(cm_w_out),
           _rows(cm_b_out))
    ssw = (bf(ssd_w_in[:, :, :d_inner]), bf(ssd_w_in[:, :, d_inner:d_inner + conv_dim]),
           bf(_pad_lanes(ssd_w_in[:, :, d_inner + conv_dim:])), ssd_conv_w, _rows(ssd_conv_b),
           _rows(_pad_lanes(ssd_dt_bias)), _rows(_pad_lanes(ssd_A_log)),
           _rows(jnp.repeat(ssd_D, head_dim, axis=1)), _rows(ssd_norm), bf(ssd_w_out))
    expand = (lax.broadcasted_iota(jnp.int32, (V7X_LANES, d_inner), 0)
              == lax.broadcasted_iota(jnp.int32, (V7X_LANES, d_inner), 1) // head_dim).astype(BF16)

    xp = x_prompt.reshape(bp * lp, d)
    xsm = x_sample.reshape(bs, d)
    pp = p_prompt.reshape(depth, bp * lp, -1)
    ps = p_sample.reshape(depth, bs, -1)
    h0_all = state_ssd.reshape(state_ssd.shape[0], bs, d_inner, d_state)
    conv_p, xbc_p, ssm_p, xbc_s = [], [], [], []
    conv_s = ssm_s = None

    for i in range(depth):
        j = i // 2
        xp = _ffn(xp, i, *ffn1)
        xsm = _ffn(xsm, i, *ffn1)
        if i % 2 == 0:
            xp3, nb = _conv_prompt(xp.reshape(bp, lp, d), i, j, nmix, *cmw)
            xp = xp3.reshape(bp * lp, d)
            conv_p.append(nb)
            xsm, conv_s = _conv_sample(xsm, state_conv, conv_s, i, j, nmix, *cmw)
        else:
            xp3, nb, hf = _ssd_prompt(xp.reshape(bp, lp, d), i, j, nmix, *ssw, dims)
            xp = xp3.reshape(bp * lp, d)
            xbc_p.append(nb)
            ssm_p.append(hf.reshape(bp, n_heads, head_dim, d_state))
            xsm, nb, ssm_s = _ssd_sample(xsm, state_ssd_conv[j].reshape(bs, -1), h0_all, ssm_s, i, j, nmix, *ssw,
                                         expand, dims)
            xbc_s.append(nb.reshape(state_ssd_conv.shape[1:]))
        last = i == depth - 1
        xp = _ffn_ple(xp, pp, i, *tail, fnorm, final_norm=last)
        xsm = _ffn_ple(xsm, ps, i, *tail, fnorm, final_norm=last)

    return (xp.reshape(bp, lp, d), xsm.reshape(bs, 1, d),
            jnp.stack(conv_p), jnp.stack(xbc_p), jnp.stack(ssm_p),
            conv_s, jnp.stack(xbc_s), ssm_s.reshape(state_ssd.shape))
```

```python
import functools

import jax
import jax.numpy as jnp
from jax import lax
from jax.experimental import pallas as pl
from jax.experimental.pallas import tpu as pltpu

F32 = jnp.float32
BF16 = jnp.bfloat16
EPS = 1e-6

V7X_LANES = 128
V7X_SUBLANES = 8
V7X_VMEM_LIMIT_BYTES = 56 * 1024 * 1024

SSD_CHUNK = 128
SSD_TILE = 512
TOKEN_TILE = 1024
CONV_TILE = 1024
CONV_ROWS = 128
SAMPLE_CONV_TILE = 32
SAMPLE_STATE_TILE = 8


def _cparams(*sem):
    return pltpu.CompilerParams(dimension_semantics=sem, vmem_limit_bytes=V7X_VMEM_LIMIT_BYTES)


def _const_spec(arr):
    nd = arr.ndim
    return pl.BlockSpec(arr.shape, lambda *_: (0,) * nd, pipeline_mode=pl.Buffered(1))


def _layer_spec(arr, layer):
    nd = arr.ndim
    return pl.BlockSpec((None,) + arr.shape[1:], lambda *_: (layer,) + (0,) * (nd - 1),
                        pipeline_mode=pl.Buffered(1))


def _dot(a, b):
    return jnp.dot(a, b, preferred_element_type=F32)


def _rmsnorm(x, g):
    return x * lax.rsqrt(jnp.mean(x * x, axis=-1, keepdims=True) + EPS) * g


def _layernorm(x, g, b):
    mu = jnp.mean(x, axis=-1, keepdims=True)
    xc = x - mu
    var = jnp.mean(xc * xc, axis=-1, keepdims=True)
    return xc * lax.rsqrt(var + EPS) * g + b


def _sigmoid(x):
    return 0.5 * jnp.tanh(0.5 * x) + 0.5


def _silu(x):
    h = 0.5 * x
    return h * jnp.tanh(h) + h


def _softplus(x):
    return jnp.maximum(x, 0.0) + jnp.log(1.0 + jnp.exp(-jnp.abs(x)))


def _swiglu(x, g, wg_ref, wu_ref, wd_ref):
    h = _rmsnorm(x, g).astype(BF16)
    gate = _dot(h, wg_ref[...])
    up = _dot(h, wu_ref[...])
    act = (_silu(gate) * up).astype(BF16)
    return _dot(act, wd_ref[...])


def _ffn_kernel(x_ref, g_ref, wg_ref, wu_ref, wd_ref, o_ref):
    x = x_ref[...]
    o_ref[...] = x + 0.5 * _swiglu(x, g_ref[...], wg_ref, wu_ref, wd_ref)


def _ffn_ple_kernel(x_ref, p_ref, g_ref, wg_ref, wu_ref, wd_ref, gp_ref, wpg_ref, wpp_ref, fn_ref, o_ref,
                    *, final_norm):
    x = x_ref[...]
    x = x + 0.5 * _swiglu(x, g_ref[...], wg_ref, wu_ref, wd_ref)
    h = _rmsnorm(x, gp_ref[...]).astype(BF16)
    gate = _sigmoid(_dot(h, wpg_ref[...]))
    x = x + gate * _dot(p_ref[...].astype(BF16), wpp_ref[...])
    if final_norm:
        x = _rmsnorm(x, fn_ref[...])
    o_ref[...] = x


def _token_tile(m):
    return TOKEN_TILE if m % TOKEN_TILE == 0 else m


def _ffn(x, layer, g, wg, wu, wd):
    m, d = x.shape
    tm = _token_tile(m)
    row = pl.BlockSpec((tm, d), lambda i: (i, 0))
    consts = (g, wg, wu, wd)
    return pl.pallas_call(
        _ffn_kernel,
        out_shape=jax.ShapeDtypeStruct((m, d), F32),
        grid=(m // tm,),
        in_specs=[row] + [_layer_spec(c, layer) for c in consts],
        out_specs=row,
        compiler_params=_cparams("parallel"),
        name="ffn",
    )(x, *consts)


def _ffn_ple(x, p, layer, g, wg, wu, wd, gp, wpg, wpp, fn, final_norm):
    m, d = x.shape
    tm = _token_tile(m)
    row = pl.BlockSpec((tm, d), lambda i: (i, 0))
    prow = pl.BlockSpec((None, tm, p.shape[2]), lambda i: (layer, i, 0))
    consts = (g, wg, wu, wd, gp, wpg, wpp)
    return pl.pallas_call(
        functools.partial(_ffn_ple_kernel, final_norm=final_norm),
        out_shape=jax.ShapeDtypeStruct((m, d), F32),
        grid=(m // tm,),
        in_specs=[row, prow] + [_layer_spec(c, layer) for c in consts] + [_const_spec(fn)],
        out_specs=row,
        compiler_params=_cparams("parallel"),
        name="ffn_ple",
    )(x, p, *consts, fn)


def _glu_in(x, g, win_ref, bin_ref):
    d = x.shape[-1]
    u = _dot(_rmsnorm(x, g).astype(BF16), win_ref[...]) + bin_ref[...]
    return u[:, :d] * _sigmoid(u[:, d:])


def _conv_tail(x, v, lng, lnb, wout_ref, bout_ref):
    v = _silu(_layernorm(v, lng, lnb)).astype(BF16)
    return x + _dot(v, wout_ref[...]) + bout_ref[...]


def _conv_prompt_kernel(x_ref, g_ref, win_ref, bin_ref, dw_ref, dwb_ref, lng_ref, lnb_ref, wout_ref, bout_ref,
                        o_ref, nbuf_ref, slab_ref, v_ref, *, taps, row_block):
    t = pl.program_id(1)
    tl = x_ref.shape[0]
    nslab = slab_ref.shape[0]
    pad = slab_ref.shape[1] - tl
    hist = taps - 1

    @pl.when(t == 0)
    def _():
        slab_ref[:, 0:pad, :] = jnp.zeros((nslab, pad, V7X_LANES), F32)

    x = x_ref[...]
    glu = _glu_in(x, g_ref[...], win_ref, bin_ref)
    for j in range(nslab):
        slab_ref[j, pad:pad + tl, :] = glu[:, j * V7X_LANES:(j + 1) * V7X_LANES]

    for j in range(nslab):
        ls = slice(j * V7X_LANES, (j + 1) * V7X_LANES)
        for r0 in range(0, tl, row_block):
            acc = jnp.broadcast_to(dwb_ref[:, ls], (row_block, V7X_LANES))
            for k in range(taps):
                lo = r0 + pad - hist + k
                acc = acc + dw_ref[k:k + 1, ls] * slab_ref[j, lo:lo + row_block, :]
            v_ref[r0:r0 + row_block, ls] = acc
    o_ref[...] = _conv_tail(x, v_ref[...], lng_ref[...], lnb_ref[...], wout_ref, bout_ref)

    @pl.when(t == pl.num_programs(1) - 1)
    def _():
        for j in range(nslab):
            nbuf_ref[:, j * V7X_LANES:(j + 1) * V7X_LANES] = slab_ref[j, pad + tl - hist:pad + tl, :]

    slab_ref[:, 0:pad, :] = slab_ref[:, tl:tl + pad, :]


def _conv_prompt(x, norm_layer, layer, g, win, b_in, dw, dwb, lng, lnb, wout, bout):
    b, l, d = x.shape
    taps = dw.shape[1]
    tl = CONV_TILE if l % CONV_TILE == 0 else l
    pad = -(-(taps - 1) // V7X_SUBLANES) * V7X_SUBLANES
    row = pl.BlockSpec((None, tl, d), lambda i, j: (i, j, 0))
    consts = (win, b_in, dw, dwb, lng, lnb, wout, bout)
    return pl.pallas_call(
        functools.partial(_conv_prompt_kernel, taps=taps, row_block=min(tl, CONV_ROWS)),
        out_shape=(jax.ShapeDtypeStruct((b, l, d), F32), jax.ShapeDtypeStruct((b, taps - 1, d), F32)),
        grid=(b, l // tl),
        in_specs=[row, _layer_spec(g, norm_layer)] + [_layer_spec(c, layer) for c in consts],
        out_specs=(row, pl.BlockSpec((None, taps - 1, d), lambda i, j: (i, 0, 0))),
        scratch_shapes=[pltpu.VMEM((d // V7X_LANES, tl + pad, V7X_LANES), F32), pltpu.VMEM((tl, d), F32)],
        compiler_params=_cparams("parallel", "arbitrary"),
        name="conv_prompt",
    )(x, g, *consts)


def _conv_sample_kernel(x_ref, buf_ref, g_ref, win_ref, bin_ref, dw_ref, dwb_ref, lng_ref, lnb_ref, wout_ref,
                        bout_ref, *rest, taps, layer):
    o_ref, nbuf_ref = rest[-2:]
    if len(nbuf_ref.shape) == 4:
        for other in range(nbuf_ref.shape[0]):
            if other != layer:
                nbuf_ref[other] = jnp.zeros(nbuf_ref.shape[1:], F32)
        nbuf_ref = nbuf_ref.at[layer]
    x = x_ref[...]
    hist = taps - 1
    glu = _glu_in(x, g_ref[...], win_ref, bin_ref)
    acc = dwb_ref[...] + dw_ref[hist:taps, :] * glu
    for k in range(hist):
        acc = acc + dw_ref[k:k + 1, :] * buf_ref[:, k, :]
    for k in range(hist - 1):
        nbuf_ref[:, k, :] = buf_ref[:, k + 1, :]
    nbuf_ref[:, hist - 1, :] = glu
    o_ref[...] = _conv_tail(x, acc, lng_ref[...], lnb_ref[...], wout_ref, bout_ref)


def _conv_sample(x, buf_all, stack, norm_layer, layer, g, win, b_in, dw, dwb, lng, lnb, wout, bout):
    m, d = x.shape
    taps = dw.shape[1]
    n_layers, _, hist, _ = buf_all.shape
    first = stack is None
    tile = SAMPLE_CONV_TILE // (n_layers if first else 1)
    tb = tile if tile and m % tile == 0 else m
    row = pl.BlockSpec((tb, d), lambda i: (i, 0))
    brow = pl.BlockSpec((None, tb, hist, d), lambda i: (layer, i, 0, 0))
    consts = (win, b_in, dw, dwb, lng, lnb, wout, bout)
    ins = [x, buf_all, g, *consts]
    in_specs = [row, brow, _layer_spec(g, norm_layer)] + [_layer_spec(c, layer) for c in consts]
    aliases = {}
    if first:
        srow = pl.BlockSpec((n_layers, tb, hist, d), lambda i: (0, i, 0, 0))
    else:
        srow = brow
        aliases = {len(ins): 1}
        ins.append(stack)
        in_specs.append(pl.BlockSpec(memory_space=pl.ANY))
    return pl.pallas_call(
        functools.partial(_conv_sample_kernel, taps=taps, layer=layer),
        out_shape=(jax.ShapeDtypeStruct((m, d), F32), jax.ShapeDtypeStruct(buf_all.shape, F32)),
        grid=(m // tb,),
        in_specs=in_specs,
        out_specs=(row, srow),
        input_output_aliases=aliases,
        compiler_params=_cparams("arbitrary"),
        name="conv_sample",
    )(*ins)


def _group_rmsnorm_gate(y, z, ng, n_groups):
    y = y * _silu(z)
    gw = y.shape[-1] // n_groups
    outs = []
    for g in range(n_groups):
        s = y[:, g * gw:(g + 1) * gw]
        outs.append(s * lax.rsqrt(jnp.mean(s * s, axis=-1, keepdims=True) + EPS))
    return jnp.concatenate(outs, axis=-1) * ng


def _prefix_sum_rows(a):
    n = a.shape[0]
    rows = lax.broadcasted_iota(jnp.int32, a.shape, 0)
    sh = 1
    while sh < n:
        a = a + jnp.where(rows >= sh, pltpu.roll(a, sh, 0), 0.0)
        sh *= 2
    return a


def _ssd_prompt_kernel(x_ref, g_ref, wz_ref, wxbc_ref, wdt_ref, cw_ref, cb_ref, dtb_ref, alog_ref, dskip_ref,
                       ng_ref, wout_ref, o_ref, nbuf_ref, hfin_ref, slab_ref, hn_ref, z_ref, dt_ref, y_ref, yn_ref,
                       ht_ref, *, n_groups, n_heads, head_dim, d_state, chunk, col_block, row_block):
    t = pl.program_id(1)
    tl = x_ref.shape[0]
    nslab = slab_ref.shape[0]
    pad = slab_ref.shape[1] - tl
    d_inner = n_heads * head_dim
    hpg = n_heads // n_groups
    gw = hpg * head_dim
    ck = cw_ref.shape[0]
    xslabs = d_inner // V7X_LANES
    assert d_state == V7X_LANES and gw % V7X_LANES == 0

    @pl.when(t == 0)
    def _():
        slab_ref[:, 0:pad, :] = jnp.zeros((nslab, pad, V7X_LANES), F32)
        ht_ref[...] = jnp.zeros(ht_ref.shape, F32)

    for r0 in range(0, tl, row_block):
        hn_ref[r0:r0 + row_block, :] = _rmsnorm(x_ref[r0:r0 + row_block, :], g_ref[...]).astype(BF16)
    hn = hn_ref[...]
    dt_ref[...] = _softplus(_dot(hn, wdt_ref[...]) + dtb_ref[...])
    for c0 in range(0, nslab * V7X_LANES, col_block):
        pre = _dot(hn, wxbc_ref[:, c0:c0 + col_block])
        for jj in range(col_block // V7X_LANES):
            slab_ref[c0 // V7X_LANES + jj, pad:pad + tl, :] = pre[:, jj * V7X_LANES:(jj + 1) * V7X_LANES]
    z_ref[...] = _dot(hn, wz_ref[...])
    for j in range(nslab):
        ls = slice(j * V7X_LANES, (j + 1) * V7X_LANES)
        tail = slab_ref[j, tl:tl + pad, :]
        for r0 in reversed(range(0, tl, row_block)):
            acc = jnp.broadcast_to(cb_ref[:, ls], (row_block, V7X_LANES))
            for k in range(ck):
                lo = r0 + pad - (ck - 1) + k
                acc = acc + cw_ref[k:k + 1, ls] * slab_ref[j, lo:lo + row_block, :]
            slab_ref[j, pad + r0:pad + r0 + row_block, :] = _silu(acc)
        slab_ref[j, 0:pad, :] = tail

    @pl.when(t == pl.num_programs(1) - 1)
    def _():
        for j in range(nslab):
            nbuf_ref[:, j * V7X_LANES:(j + 1) * V7X_LANES] = slab_ref[j, pad - (ck - 1):pad, :]

    neg_a = jnp.exp(alog_ref[...])
    hps = V7X_LANES // head_dim
    lane = lax.broadcasted_iota(jnp.int32, (1, V7X_LANES), 1)
    causal = lax.broadcasted_iota(jnp.int32, (chunk, chunk), 0) >= lax.broadcasted_iota(jnp.int32, (chunk, chunk), 1)

    def chunk_body(c, carry):
        r0 = pl.multiple_of(c * chunk, chunk)
        rows = pl.ds(pl.multiple_of(pad + c * chunk, V7X_SUBLANES), chunk)
        dt = dt_ref[pl.ds(r0, chunk), :]
        acum = _prefix_sum_rows(-dt * neg_a)
        a_last = acum[chunk - 1:chunk, :]
        acum_t = acum.T
        dt_t = dt.T
        decdt_t = (dt * jnp.exp(a_last - acum)).T
        cdec = jnp.exp(a_last)
        for g in range(n_groups):
            bg = slab_ref[xslabs + g, rows, :]
            cg = slab_ref[xslabs + n_groups + g, rows, :]
            cbm = lax.dot_general(cg.astype(BF16), bg.astype(BF16), (((1,), (1,)), ((), ())),
                                  preferred_element_type=F32)
            bg_t = bg.T
            for sl in range(g * gw // V7X_LANES, (g + 1) * gw // V7X_LANES):
                ls = slice(sl * V7X_LANES, (sl + 1) * V7X_LANES)
                xs_sl = slab_ref[sl, rows, :]
                hprev = ht_ref[:, ls]
                y_sl = upd_sl = cdec_sl = None
                for u in range(hps):
                    h = sl * hps + u
                    mine = (lane >= u * head_dim) & (lane < (u + 1) * head_dim)
                    xm = jnp.where(mine, xs_sl, 0.0).astype(BF16)
                    hm = jnp.where(mine, hprev, 0.0).astype(BF16)
                    colb = jnp.broadcast_to(acum[:, h:h + 1], (chunk, chunk))
                    lmat = jnp.exp(jnp.where(causal, colb - acum_t[h:h + 1, :], -jnp.inf))
                    lhs = jnp.concatenate([cbm * lmat * dt_t[h:h + 1, :], cg * jnp.exp(colb)], axis=1)
                    yh = _dot(lhs.astype(BF16), jnp.concatenate([xm, hm], axis=0))
                    uh = _dot((bg_t * decdt_t[h:h + 1, :]).astype(BF16), xm)
                    cd = jnp.where(mine, cdec[:, h:h + 1], 0.0)
                    y_sl = yh if u == 0 else y_sl + yh
                    upd_sl = uh if u == 0 else upd_sl + uh
                    cdec_sl = cd if u == 0 else cdec_sl + cd
                y_ref[pl.ds(r0, chunk), ls] = y_sl
                ht_ref[:, ls] = hprev * cdec_sl + upd_sl
        return carry

    lax.fori_loop(0, tl // chunk, chunk_body, 0, unroll=2)

    out = x_ref[...]
    for g in range(n_groups):
        gs = slice(g * gw, (g + 1) * gw)
        for r0 in range(0, tl, row_block):
            rs = slice(r0, r0 + row_block)
            xs = jnp.concatenate([slab_ref[j, pad + r0:pad + r0 + row_block, :]
                                  for j in range(g * gw // V7X_LANES, (g + 1) * gw // V7X_LANES)], axis=1)
            yg = (y_ref[rs, gs] + dskip_ref[:, gs] * xs) * _silu(z_ref[rs, gs])
            yg = yg * lax.rsqrt(jnp.mean(yg * yg, axis=-1, keepdims=True) + EPS) * ng_ref[:, gs]
            yn_ref[rs, gs] = yg.astype(BF16)
        out = out + _dot(yn_ref[:, gs], wout_ref[gs, :])
    o_ref[...] = out

    @pl.when(t == pl.num_programs(1) - 1)
    def _():
        hfin_ref[...] = ht_ref[...].T


def _ssd_prompt(x, norm_layer, layer, g, wz, wxbc, wdt, cw, cb, dtb, alog, dskip, ng, wout, dims):
    b, l, d = x.shape
    n_groups, n_heads, head_dim, d_state = dims
    d_inner = n_heads * head_dim
    cd = wxbc.shape[2]
    ck = cw.shape[1]
    tl = SSD_TILE if l % SSD_TILE == 0 else l
    q = SSD_CHUNK if tl % SSD_CHUNK == 0 else tl
    row = pl.BlockSpec((None, tl, d), lambda i, j: (i, j, 0))
    consts = (wz, wxbc, wdt, cw, cb, dtb, alog, dskip, ng, wout)
    return pl.pallas_call(
        functools.partial(_ssd_prompt_kernel, n_groups=n_groups, n_heads=n_heads, head_dim=head_dim,
                          d_state=d_state, chunk=q, col_block=min(cd, 2 * V7X_LANES), row_block=q),
        out_shape=(jax.ShapeDtypeStruct((b, l, d), F32),
                   jax.ShapeDtypeStruct((b, ck - 1, cd), F32),
                   jax.ShapeDtypeStruct((b, d_inner, d_state), F32)),
        grid=(b, l // tl),
        in_specs=[row, _layer_spec(g, norm_layer)] + [_layer_spec(c, layer) for c in consts],
        out_specs=(row,
                   pl.BlockSpec((None, ck - 1, cd), lambda i, j: (i, 0, 0)),
                   pl.BlockSpec((None, d_inner, d_state), lambda i, j: (i, 0, 0))),
        scratch_shapes=[pltpu.VMEM((cd // V7X_LANES, tl + V7X_SUBLANES, V7X_LANES), F32),
                        pltpu.VMEM((tl, d), BF16),
                        pltpu.VMEM((tl, d_inner), F32),
                        pltpu.VMEM((tl, V7X_LANES), F32),
                        pltpu.VMEM((tl, d_inner), F32),
                        pltpu.VMEM((tl, d_inner), BF16),
                        pltpu.VMEM((d_state, d_inner), F32)],
        compiler_params=_cparams("parallel", "arbitrary"),
        name="ssd_prompt",
    )(x, g, *consts)


def _expand_heads(v, e_ref):
    hi = v.astype(BF16)
    r1 = v - hi.astype(F32)
    mid = r1.astype(BF16)
    lo = (r1 - mid.astype(F32)).astype(BF16)
    e = e_ref[...]
    return _dot(hi, e) + _dot(mid, e) + _dot(lo, e)


def _ssd_sample_in_kernel(x_ref, buf_ref, g_ref, wz_ref, wxbc_ref, wdt_ref, cw_ref, cb_ref, dtb_ref, alog_ref,
                          e_ref, *rest, d_inner, gn, layer):
    z_ref, xs_ref, bm_ref, cmt_ref, xdtt_ref, expa_ref, nbuf_ref = rest[-7:]
    if len(nbuf_ref.shape) == 3:
        for other in range(nbuf_ref.shape[0]):
            if other != layer:
                nbuf_ref[other] = jnp.zeros(nbuf_ref.shape[1:], F32)
        nbuf_ref = nbuf_ref.at[layer]
    x = x_ref[...]
    hn = _rmsnorm(x, g_ref[...]).astype(BF16)
    z_ref[...] = _dot(hn, wz_ref[...])
    new = _dot(hn, wxbc_ref[...])
    cd = new.shape[1]
    ck = cw_ref.shape[0]
    xbc = cb_ref[...] + cw_ref[ck - 1:ck, :] * new
    for k in range(ck - 1):
        xbc = xbc + cw_ref[k:k + 1, :] * buf_ref[:, k * cd:(k + 1) * cd]
    nbuf_ref[:, 0:(ck - 2) * cd] = buf_ref[:, cd:(ck - 1) * cd]
    nbuf_ref[:, (ck - 2) * cd:(ck - 1) * cd] = new
    xbc = _silu(xbc)
    xs = xbc[:, :d_inner]
    dt = _softplus(_dot(hn, wdt_ref[...]) + dtb_ref[...])
    xs_ref[...] = xs
    bm_ref[...] = xbc[:, d_inner:d_inner + gn]
    cmt_ref[...] = xbc[:, d_inner + gn:].T
    xdtt_ref[...] = (xs * _expand_heads(dt, e_ref)).T
    expa_ref[...] = jnp.exp(dt * (-jnp.exp(alog_ref[...])))


def _ssd_sample_state_kernel(h0_ref, bm_ref, cmt_ref, xdtt_ref, expa_ref, *rest,
                             n_groups, n_heads, head_dim, d_state, layer):
    hout_ref, yt_ref = rest[-2:]
    t = pl.program_id(0)
    tb = h0_ref.shape[0]
    ntok = bm_ref.shape[0]
    hpg = n_heads // n_groups
    gw = hpg * head_dim
    if len(hout_ref.shape) == 4:
        for other in range(hout_ref.shape[0]):
            if other != layer:
                hout_ref[other] = jnp.zeros(hout_ref.shape[1:], F32)
        hout_ref = hout_ref.at[layer]

    @pl.when(t == 0)
    def _():
        yt_ref[...] = jnp.zeros(yt_ref.shape, F32)

    tok_rows = lax.broadcasted_iota(jnp.int32, (ntok, d_state), 0)
    tok_cols = lax.broadcasted_iota(jnp.int32, (d_state, ntok), 1)

    def body(i, carry):
        tok = t * tb + i
        ea = expa_ref[pl.ds(tok, 1), :]
        for g in range(n_groups):
            rs = slice(g * gw, (g + 1) * gw)
            ns = slice(g * d_state, (g + 1) * d_state)
            rb = jnp.where(tok_rows == tok, bm_ref[:, ns], 0.0).astype(BF16)
            upd = _dot(xdtt_ref[rs, :].astype(BF16), rb)
            parts = []
            for r in range(hpg):
                h = g * hpg + r
                hr = slice(h * head_dim, (h + 1) * head_dim)
                parts.append(h0_ref[i, hr, :] * ea[:, h:h + 1])
            hnew = jnp.concatenate(parts, axis=0) + upd
            hout_ref[i, rs, :] = hnew
            rc = jnp.where(tok_cols == tok, cmt_ref[ns, :], 0.0).astype(BF16)
            yt_ref[rs, :] = yt_ref[rs, :] + _dot(hnew.astype(BF16), rc)
        return carry

    lax.fori_loop(0, tb, body, 0)


def _ssd_sample_out_kernel(x_ref, yt_ref, xs_ref, z_ref, dskip_ref, ng_ref, wout_ref, o_ref, *, n_groups):
    y = yt_ref[...].T + dskip_ref[...] * xs_ref[...]
    yn = _group_rmsnorm_gate(y, z_ref[...], ng_ref[...], n_groups).astype(BF16)
    o_ref[...] = x_ref[...] + _dot(yn, wout_ref[...])


def _ssd_sample(x, buf_all, nstack, h0_all, hstack, norm_layer, layer, g, wz, wxbc, wdt, cw, cb, dtb, alog, dskip,
                ng, wout, e, dims):
    m, d = x.shape
    n_groups, n_heads, head_dim, d_state = dims
    d_inner = n_heads * head_dim
    gn = n_groups * d_state
    consts = (wz, wxbc, wdt, cw, cb, dtb, alog)
    full = lambda a: pl.BlockSpec(a.shape, lambda *_: (0,) * a.ndim)
    outs = (jax.ShapeDtypeStruct((m, d_inner), F32),
            jax.ShapeDtypeStruct((m, d_inner), F32),
            jax.ShapeDtypeStruct((m, gn), F32),
            jax.ShapeDtypeStruct((gn, m), F32),
            jax.ShapeDtypeStruct((d_inner, m), F32),
            jax.ShapeDtypeStruct((m, V7X_LANES), F32),
            jax.ShapeDtypeStruct(buf_all.shape, F32))
    first = hstack is None
    ins = [x, buf_all, g, *consts, e]
    in_specs = ([full(x), _layer_spec(buf_all, layer), _layer_spec(g, norm_layer)]
                + [_layer_spec(c, layer) for c in consts] + [_const_spec(e)])
    out_specs = [pl.BlockSpec(o.shape, lambda *_, n=len(o.shape): (0,) * n) for o in outs]
    aliases = {}
    if not first:
        out_specs[-1] = pl.BlockSpec((None,) + buf_all.shape[1:], lambda *_: (layer, 0, 0))
        aliases = {len(ins): len(outs) - 1}
        ins.append(nstack)
        in_specs.append(pl.BlockSpec(memory_space=pl.ANY))
    z, xs, bm, cmt, xdtt, expa, nstack = pl.pallas_call(
        functools.partial(_ssd_sample_in_kernel, d_inner=d_inner, gn=gn, layer=layer),
        out_shape=outs,
        grid=(1,),
        in_specs=in_specs,
        out_specs=tuple(out_specs),
        input_output_aliases=aliases,
        compiler_params=_cparams("arbitrary"),
        name="ssd_sample_in",
    )(*ins)

    tile = SAMPLE_STATE_TILE // (h0_all.shape[0] if first else 1)
    tb = tile if tile and m % tile == 0 else m
    st = pl.BlockSpec((None, tb, d_inner, d_state), lambda i: (layer, i, 0, 0))
    ins = [h0_all, bm, cmt, xdtt, expa]
    in_specs = [st, _const_spec(bm), _const_spec(cmt), _const_spec(xdtt), _const_spec(expa)]
    aliases = {}
    if first:
        st_out = pl.BlockSpec((h0_all.shape[0], tb, d_inner, d_state), lambda i: (0, i, 0, 0))
    else:
        st_out = st
        aliases = {len(ins): 0}
        ins.append(hstack)
        in_specs.append(pl.BlockSpec(memory_space=pl.ANY))
    hstack, yt = pl.pallas_call(
        functools.partial(_ssd_sample_state_kernel, n_groups=n_groups, n_heads=n_heads, head_dim=head_dim,
                          d_state=d_state, layer=layer),
        out_shape=(jax.ShapeDtypeStruct(h0_all.shape, F32), jax.ShapeDtypeStruct((d_inner, m), F32)),
        grid=(m // tb,),
        in_specs=in_specs,
        out_specs=(st_out, pl.BlockSpec((d_inner, m), lambda i: (0, 0))),
        input_output_aliases=aliases,
        compiler_params=_cparams("arbitrary"),
        name="ssd_sample_state",
    )(*ins)

    out = pl.pallas_call(
        functools.partial(_ssd_sample_out_kernel, n_groups=n_groups),
        out_shape=jax.ShapeDtypeStruct((m, d), F32),
        grid=(1,),
        in_specs=[full(x), full(yt), full(xs), full(z), _layer_spec(dskip, layer), _layer_spec(ng, layer),
                  _layer_spec(wout, layer)],
        out_specs=full(x),
        compiler_params=_cparams("arbitrary"),
        name="ssd_sample_out",
    )(x, yt, xs, z, dskip, ng, wout)
    return out, nstack, hstack


def _rows(v):
    return v.reshape(v.shape[0], 1, v.shape[1]).astype(F32)


def _pad_lanes(v):
    return jnp.pad(v, [(0, 0)] * (v.ndim - 1) + [(0, V7X_LANES - v.shape[-1])])


def kernel(x_prompt, x_sample, state_conv, state_ssd_conv, state_ssd, p_prompt, p_sample, norm_ffn1, w_ffn1_gate, w_ffn1_up, w_ffn1_down, norm_mix, norm_ffn2, w_ffn2_gate, w_ffn2_up, w_ffn2_down, norm_ple, w_ple_gate, w_ple_proj, cm_w_in, cm_b_in, cm_dw, cm_dw_b, cm_ln_g, cm_ln_b, cm_w_out, cm_b_out, ssd_w_in, ssd_conv_w, ssd_conv_b, ssd_dt_bias, ssd_A_log, ssd_D, ssd_norm, ssd_w_out, final_norm):
    depth = norm_ffn1.shape[0]
    bp, lp, d = x_prompt.shape
    bs = x_sample.shape[0]
    n_heads = ssd_dt_bias.shape[1]
    head_dim, d_state = state_ssd.shape[3], state_ssd.shape[4]
    d_inner = n_heads * head_dim
    conv_dim = ssd_conv_w.shape[2]
    n_groups = (conv_dim - d_inner) // (2 * d_state)
    dims = (n_groups, n_heads, head_dim, d_state)
    assert n_heads <= V7X_LANES and lp % SSD_CHUNK == 0 and x_sample.shape[1] == 1

    bf = lambda w: w.astype(BF16)
    ffn1 = (_rows(norm_ffn1), bf(w_ffn1_gate), bf(w_ffn1_up), bf(w_ffn1_down))
    tail = (_rows(norm_ffn2), bf(w_ffn2_gate), bf(w_ffn2_up), bf(w_ffn2_down), _rows(norm_ple), bf(w_ple_gate),
            bf(w_ple_proj))
    fnorm = final_norm.reshape(1, d)
    nmix = _rows(norm_mix)
    cmw = (bf(cm_w_in), _rows(cm_b_in), cm_dw, _rows(cm_dw_b), _rows(cm_ln_g), _rows(cm_ln_b), bf(cm_w_out),
           _rows(cm_b_out))
    ssw = (bf(ssd_w_in[:, :, :d_inner]), bf(ssd_w_in[:, :, d_inner:d_inner + conv_dim]),
           bf(_pad_lanes(ssd_w_in[:, :, d_inner + conv_dim:])), ssd_conv_w, _rows(ssd_conv_b),
           _rows(_pad_lanes(ssd_dt_bias)), _rows(_pad_lanes(ssd_A_log)),
           _rows(jnp.repeat(ssd_D, head_dim, axis=1)), _rows(ssd_norm), bf(ssd_w_out))
    expand = (lax.broadcasted_iota(jnp.int32, (V7X_LANES, d_inner), 0)
              == lax.broadcasted_iota(jnp.int32, (V7X_LANES, d_inner), 1) // head_dim).astype(BF16)

    xp = x_prompt.reshape(bp * lp, d)
    xsm = x_sample.reshape(bs, d)
    pp = p_prompt.reshape(depth, bp * lp, -1)
    ps = p_sample.reshape(depth, bs, -1)
    h0_all = state_ssd.reshape(state_ssd.shape[0], bs, d_inner, d_state)
    hist_all = state_ssd_conv.reshape(state_ssd_conv.shape[0], bs, -1)
    conv_p, xbc_p, ssm_p = [], [], []
    conv_s = xbc_s = ssm_s = None

    for i in range(depth):
        j = i // 2
        xp = _ffn(xp, i, *ffn1)
        xsm = _ffn(xsm, i, *ffn1)
        if i % 2 == 0:
            xp3, nb = _conv_prompt(xp.reshape(bp, lp, d), i, j, nmix, *cmw)
            xp = xp3.reshape(bp * lp, d)
            conv_p.append(nb)
            xsm, conv_s = _conv_sample(xsm, state_conv, conv_s, i, j, nmix, *cmw)
        else:
            xp3, nb, hf = _ssd_prompt(xp.reshape(bp, lp, d), i, j, nmix, *ssw, dims)
            xp = xp3.reshape(bp * lp, d)
            xbc_p.append(nb)
            ssm_p.append(hf.reshape(bp, n_heads, head_dim, d_state))
            xsm, xbc_s, ssm_s = _ssd_sample(xsm, hist_all, xbc_s, h0_all, ssm_s, i, j, nmix, *ssw, expand, dims)
        last = i == depth - 1
        xp = _ffn_ple(xp, pp, i, *tail, fnorm, final_norm=last)
        xsm = _ffn_ple(xsm, ps, i, *tail, fnorm, final_norm=last)

    return (xp.reshape(bp, lp, d), xsm.reshape(bs, 1, d),
            jnp.stack(conv_p), jnp.stack(xbc_p), jnp.stack(ssm_p),
            conv_s, xbc_s.reshape(state_ssd_conv.shape), ssm_s.reshape(state_ssd.shape))
```

```python
import functools

import jax
import jax.numpy as jnp
from jax import lax
from jax.experimental import pallas as pl
from jax.experimental.pallas import tpu as pltpu

F32 = jnp.float32
BF16 = jnp.bfloat16
EPS = 1e-6

V7X_LANES = 128
V7X_SUBLANES = 8
V7X_VMEM_LIMIT_BYTES = 56 * 1024 * 1024

SSD_CHUNK = 128
SSD_TILE = 512
TOKEN_TILE = 1024
CONV_TILE = 1024
CONV_ROWS = 128
SAMPLE_CONV_TILE = 32
SAMPLE_STATE_TILE = 8


def _cparams(*sem):
    return pltpu.CompilerParams(dimension_semantics=sem, vmem_limit_bytes=V7X_VMEM_LIMIT_BYTES)


def _const_spec(arr):
    nd = arr.ndim
    return pl.BlockSpec(arr.shape, lambda *_: (0,) * nd, pipeline_mode=pl.Buffered(1))


def _layer_spec(arr, layer):
    nd = arr.ndim
    return pl.BlockSpec((None,) + arr.shape[1:], lambda *_: (layer,) + (0,) * (nd - 1),
                        pipeline_mode=pl.Buffered(1))


def _dot(a, b):
    return jnp.dot(a, b, preferred_element_type=F32)


def _rmsnorm(x, g):
    return x * lax.rsqrt(jnp.mean(x * x, axis=-1, keepdims=True) + EPS) * g


def _layernorm(x, g, b):
    mu = jnp.mean(x, axis=-1, keepdims=True)
    xc = x - mu
    var = jnp.mean(xc * xc, axis=-1, keepdims=True)
    return xc * lax.rsqrt(var + EPS) * g + b


def _sigmoid(x):
    return 0.5 * jnp.tanh(0.5 * x) + 0.5


def _silu(x):
    h = 0.5 * x
    return h * jnp.tanh(h) + h


def _softplus(x):
    return jnp.maximum(x, 0.0) + jnp.log(1.0 + jnp.exp(-jnp.abs(x)))


def _swiglu(x, g, wg_ref, wu_ref, wd_ref):
    h = _rmsnorm(x, g).astype(BF16)
    gate = _dot(h, wg_ref[...])
    up = _dot(h, wu_ref[...])
    act = (_silu(gate) * up).astype(BF16)
    return _dot(act, wd_ref[...])


def _ffn_kernel(x_ref, g_ref, wg_ref, wu_ref, wd_ref, o_ref):
    x = x_ref[...]
    o_ref[...] = x + 0.5 * _swiglu(x, g_ref[...], wg_ref, wu_ref, wd_ref)


def _ffn_ple_kernel(x_ref, p_ref, g_ref, wg_ref, wu_ref, wd_ref, gp_ref, wpg_ref, wpp_ref, fn_ref, o_ref,
                    *, final_norm):
    x = x_ref[...]
    x = x + 0.5 * _swiglu(x, g_ref[...], wg_ref, wu_ref, wd_ref)
    h = _rmsnorm(x, gp_ref[...]).astype(BF16)
    gate = _sigmoid(_dot(h, wpg_ref[...]))
    x = x + gate * _dot(p_ref[...].astype(BF16), wpp_ref[...])
    if final_norm:
        x = _rmsnorm(x, fn_ref[...])
    o_ref[...] = x


def _token_tile(m):
    return TOKEN_TILE if m % TOKEN_TILE == 0 else m


def _ffn(x, layer, g, wg, wu, wd):
    m, d = x.shape
    tm = _token_tile(m)
    row = pl.BlockSpec((tm, d), lambda i: (i, 0))
    consts = (g, wg, wu, wd)
    return pl.pallas_call(
        _ffn_kernel,
        out_shape=jax.ShapeDtypeStruct((m, d), F32),
        grid=(m // tm,),
        in_specs=[row] + [_layer_spec(c, layer) for c in consts],
        out_specs=row,
        compiler_params=_cparams("parallel"),
        name="ffn",
    )(x, *consts)


def _ffn_ple(x, p, layer, g, wg, wu, wd, gp, wpg, wpp, fn, final_norm):
    m, d = x.shape
    tm = _token_tile(m)
    row = pl.BlockSpec((tm, d), lambda i: (i, 0))
    prow = pl.BlockSpec((None, tm, p.shape[2]), lambda i: (layer, i, 0))
    consts = (g, wg, wu, wd, gp, wpg, wpp)
    return pl.pallas_call(
        functools.partial(_ffn_ple_kernel, final_norm=final_norm),
        out_shape=jax.ShapeDtypeStruct((m, d), F32),
        grid=(m // tm,),
        in_specs=[row, prow] + [_layer_spec(c, layer) for c in consts] + [_const_spec(fn)],
        out_specs=row,
        compiler_params=_cparams("parallel"),
        name="ffn_ple",
    )(x, p, *consts, fn)


def _glu_in(x, g, win_ref, bin_ref):
    d = x.shape[-1]
    u = _dot(_rmsnorm(x, g).astype(BF16), win_ref[...]) + bin_ref[...]
    return u[:, :d] * _sigmoid(u[:, d:])


def _conv_tail(x, v, lng, lnb, wout_ref, bout_ref):
    v = _silu(_layernorm(v, lng, lnb)).astype(BF16)
    return x + _dot(v, wout_ref[...]) + bout_ref[...]


def _conv_prompt_kernel(x_ref, g_ref, win_ref, bin_ref, dw_ref, dwb_ref, lng_ref, lnb_ref, wout_ref, bout_ref,
                        o_ref, nbuf_ref, slab_ref, v_ref, *, taps, row_block):
    t = pl.program_id(1)
    tl = x_ref.shape[0]
    nslab = slab_ref.shape[0]
    pad = slab_ref.shape[1] - tl
    hist = taps - 1

    @pl.when(t == 0)
    def _():
        slab_ref[:, 0:pad, :] = jnp.zeros((nslab, pad, V7X_LANES), F32)

    x = x_ref[...]
    glu = _glu_in(x, g_ref[...], win_ref, bin_ref)
    for j in range(nslab):
        slab_ref[j, pad:pad + tl, :] = glu[:, j * V7X_LANES:(j + 1) * V7X_LANES]

    for j in range(nslab):
        ls = slice(j * V7X_LANES, (j + 1) * V7X_LANES)
        for r0 in range(0, tl, row_block):
            acc = jnp.broadcast_to(dwb_ref[:, ls], (row_block, V7X_LANES))
            for k in range(taps):
                lo = r0 + pad - hist + k
                acc = acc + dw_ref[k:k + 1, ls] * slab_ref[j, lo:lo + row_block, :]
            v_ref[r0:r0 + row_block, ls] = acc
    o_ref[...] = _conv_tail(x, v_ref[...], lng_ref[...], lnb_ref[...], wout_ref, bout_ref)

    @pl.when(t == pl.num_programs(1) - 1)
    def _():
        for j in range(nslab):
            nbuf_ref[:, j * V7X_LANES:(j + 1) * V7X_LANES] = slab_ref[j, pad + tl - hist:pad + tl, :]

    slab_ref[:, 0:pad, :] = slab_ref[:, tl:tl + pad, :]


def _conv_prompt(x, norm_layer, layer, g, win, b_in, dw, dwb, lng, lnb, wout, bout):
    b, l, d = x.shape
    taps = dw.shape[1]
    tl = CONV_TILE if l % CONV_TILE == 0 else l
    pad = -(-(taps - 1) // V7X_SUBLANES) * V7X_SUBLANES
    row = pl.BlockSpec((None, tl, d), lambda i, j: (i, j, 0))
    consts = (win, b_in, dw, dwb, lng, lnb, wout, bout)
    return pl.pallas_call(
        functools.partial(_conv_prompt_kernel, taps=taps, row_block=min(tl, CONV_ROWS)),
        out_shape=(jax.ShapeDtypeStruct((b, l, d), F32), jax.ShapeDtypeStruct((b, taps - 1, d), F32)),
        grid=(b, l // tl),
        in_specs=[row, _layer_spec(g, norm_layer)] + [_layer_spec(c, layer) for c in consts],
        out_specs=(row, pl.BlockSpec((None, taps - 1, d), lambda i, j: (i, 0, 0))),
        scratch_shapes=[pltpu.VMEM((d // V7X_LANES, tl + pad, V7X_LANES), F32), pltpu.VMEM((tl, d), F32)],
        compiler_params=_cparams("parallel", "arbitrary"),
        name="conv_prompt",
    )(x, g, *consts)


def _conv_sample_kernel(x_ref, buf_ref, g_ref, win_ref, bin_ref, dw_ref, dwb_ref, lng_ref, lnb_ref, wout_ref,
                        bout_ref, *rest, taps, layer):
    o_ref, nbuf_ref = rest[-2:]
    if len(nbuf_ref.shape) == 4:
        for other in range(nbuf_ref.shape[0]):
            if other != layer:
                nbuf_ref[other] = jnp.zeros(nbuf_ref.shape[1:], F32)
        nbuf_ref = nbuf_ref.at[layer]
    x = x_ref[...]
    hist = taps - 1
    glu = _glu_in(x, g_ref[...], win_ref, bin_ref)
    acc = dwb_ref[...] + dw_ref[hist:taps, :] * glu
    for k in range(hist):
        acc = acc + dw_ref[k:k + 1, :] * buf_ref[:, k, :]
    for k in range(hist - 1):
        nbuf_ref[:, k, :] = buf_ref[:, k + 1, :]
    nbuf_ref[:, hist - 1, :] = glu
    o_ref[...] = _conv_tail(x, acc, lng_ref[...], lnb_ref[...], wout_ref, bout_ref)


def _conv_sample(x, buf_all, stack, norm_layer, layer, g, win, b_in, dw, dwb, lng, lnb, wout, bout):
    m, d = x.shape
    taps = dw.shape[1]
    n_layers, _, hist, _ = buf_all.shape
    first = stack is None
    tile = SAMPLE_CONV_TILE // (n_layers if first else 1)
    tb = tile if tile and m % tile == 0 else m
    row = pl.BlockSpec((tb, d), lambda i: (i, 0))
    brow = pl.BlockSpec((None, tb, hist, d), lambda i: (layer, i, 0, 0))
    consts = (win, b_in, dw, dwb, lng, lnb, wout, bout)
    ins = [x, buf_all, g, *consts]
    in_specs = [row, brow, _layer_spec(g, norm_layer)] + [_layer_spec(c, layer) for c in consts]
    aliases = {}
    if first:
        srow = pl.BlockSpec((n_layers, tb, hist, d), lambda i: (0, i, 0, 0))
    else:
        srow = brow
        aliases = {len(ins): 1}
        ins.append(stack)
        in_specs.append(pl.BlockSpec(memory_space=pl.ANY))
    return pl.pallas_call(
        functools.partial(_conv_sample_kernel, taps=taps, layer=layer),
        out_shape=(jax.ShapeDtypeStruct((m, d), F32), jax.ShapeDtypeStruct(buf_all.shape, F32)),
        grid=(m // tb,),
        in_specs=in_specs,
        out_specs=(row, srow),
        input_output_aliases=aliases,
        compiler_params=_cparams("arbitrary"),
        name="conv_sample",
    )(*ins)


def _group_rmsnorm_gate(y, z, ng, n_groups):
    y = y * _silu(z)
    gw = y.shape[-1] // n_groups
    outs = []
    for g in range(n_groups):
        s = y[:, g * gw:(g + 1) * gw]
        outs.append(s * lax.rsqrt(jnp.mean(s * s, axis=-1, keepdims=True) + EPS))
    return jnp.concatenate(outs, axis=-1) * ng


def _prefix_sum_rows(a):
    n = a.shape[0]
    rows = lax.broadcasted_iota(jnp.int32, a.shape, 0)
    sh = 1
    while sh < n:
        a = a + jnp.where(rows >= sh, pltpu.roll(a, sh, 0), 0.0)
        sh *= 2
    return a


def _ssd_prompt_kernel(x_ref, g_ref, wz_ref, wxbc_ref, wdt_ref, cw_ref, cb_ref, dtb_ref, alog_ref, dskip_ref,
                       ng_ref, wout_ref, o_ref, nbuf_ref, hfin_ref, slab_ref, hn_ref, z_ref, dt_ref, y_ref, yn_ref,
                       ht_ref, *, n_groups, n_heads, head_dim, d_state, chunk, col_block, row_block):
    t = pl.program_id(1)
    tl = x_ref.shape[0]
    nslab = slab_ref.shape[0]
    pad = slab_ref.shape[1] - tl
    d_inner = n_heads * head_dim
    hpg = n_heads // n_groups
    gw = hpg * head_dim
    ck = cw_ref.shape[0]
    xslabs = d_inner // V7X_LANES
    assert d_state == V7X_LANES and gw % V7X_LANES == 0 and chunk == d_state

    @pl.when(t == 0)
    def _():
        slab_ref[:, 0:pad, :] = jnp.zeros((nslab, pad, V7X_LANES), F32)
        ht_ref[...] = jnp.zeros(ht_ref.shape, F32)

    for r0 in range(0, tl, row_block):
        hn_ref[r0:r0 + row_block, :] = _rmsnorm(x_ref[r0:r0 + row_block, :], g_ref[...]).astype(BF16)
    hn = hn_ref[...]
    dt_ref[...] = _softplus(_dot(hn, wdt_ref[...]) + dtb_ref[...])
    for c0 in range(0, nslab * V7X_LANES, col_block):
        pre = _dot(hn, wxbc_ref[:, c0:c0 + col_block])
        for jj in range(col_block // V7X_LANES):
            slab_ref[c0 // V7X_LANES + jj, pad:pad + tl, :] = pre[:, jj * V7X_LANES:(jj + 1) * V7X_LANES]
    z_ref[...] = _dot(hn, wz_ref[...])
    for j in range(nslab):
        ls = slice(j * V7X_LANES, (j + 1) * V7X_LANES)
        tail = slab_ref[j, tl:tl + pad, :]
        half_w = [0.5 * cw_ref[k:k + 1, ls] for k in range(ck)]
        half_b = 0.5 * cb_ref[:, ls]
        for r0 in reversed(range(0, tl, row_block)):
            h = jnp.broadcast_to(half_b, (row_block, V7X_LANES))
            for k in range(ck):
                lo = r0 + pad - (ck - 1) + k
                h = h + half_w[k] * slab_ref[j, lo:lo + row_block, :]
            slab_ref[j, pad + r0:pad + r0 + row_block, :] = h * jnp.tanh(h) + h
        slab_ref[j, 0:pad, :] = tail

    @pl.when(t == pl.num_programs(1) - 1)
    def _():
        for j in range(nslab):
            nbuf_ref[:, j * V7X_LANES:(j + 1) * V7X_LANES] = slab_ref[j, pad - (ck - 1):pad, :]

    neg_a = jnp.exp(alog_ref[...])
    hps = V7X_LANES // head_dim
    lane = lax.broadcasted_iota(jnp.int32, (1, V7X_LANES), 1)
    causal = lax.broadcasted_iota(jnp.int32, (chunk, chunk), 0) >= lax.broadcasted_iota(jnp.int32, (chunk, chunk), 1)

    def chunk_body(c, carry):
        r0 = pl.multiple_of(c * chunk, chunk)
        rows = pl.ds(pl.multiple_of(pad + c * chunk, V7X_SUBLANES), chunk)
        dt = dt_ref[pl.ds(r0, chunk), :]
        acum = _prefix_sum_rows(-dt * neg_a)
        a_last = acum[chunk - 1:chunk, :]
        acum_t = acum.T
        dt_t = dt.T
        decdt_t = (dt * jnp.exp(a_last - acum)).T
        cdec = jnp.exp(a_last)
        for g in range(n_groups):
            bg = slab_ref[xslabs + g, rows, :]
            cg = slab_ref[xslabs + n_groups + g, rows, :]
            cbm = lax.dot_general(cg.astype(BF16), bg.astype(BF16), (((1,), (1,)), ((), ())),
                                  preferred_element_type=F32)
            bg_t = bg.T
            for sl in range(g * gw // V7X_LANES, (g + 1) * gw // V7X_LANES):
                ls = slice(sl * V7X_LANES, (sl + 1) * V7X_LANES)
                xs_sl = slab_ref[sl, rows, :]
                hprev = ht_ref[:, ls]
                y_sl = upd_sl = cdec_sl = None
                for u in range(hps):
                    h = sl * hps + u
                    mine = (lane >= u * head_dim) & (lane < (u + 1) * head_dim)
                    xm = jnp.where(mine, xs_sl, 0.0).astype(BF16)
                    hm = jnp.where(mine, hprev, 0.0).astype(BF16)
                    colb = jnp.broadcast_to(acum[:, h:h + 1], (chunk, chunk))
                    lmat = jnp.exp(jnp.where(causal, colb - acum_t[h:h + 1, :], -jnp.inf))
                    lhs = jnp.concatenate([cbm * lmat * dt_t[h:h + 1, :], cg * jnp.exp(colb)], axis=1)
                    yh = _dot(lhs.astype(BF16), jnp.concatenate([xm, hm], axis=0))
                    uh = _dot((bg_t * decdt_t[h:h + 1, :]).astype(BF16), xm)
                    cd = jnp.where(mine, cdec[:, h:h + 1], 0.0)
                    y_sl = yh if u == 0 else y_sl + yh
                    upd_sl = uh if u == 0 else upd_sl + uh
                    cdec_sl = cd if u == 0 else cdec_sl + cd
                y_ref[pl.ds(r0, chunk), ls] = y_sl
                ht_ref[:, ls] = hprev * cdec_sl + upd_sl
        return carry

    lax.fori_loop(0, tl // chunk, chunk_body, 0, unroll=2)

    out = x_ref[...]
    for g in range(n_groups):
        gs = slice(g * gw, (g + 1) * gw)
        for r0 in range(0, tl, row_block):
            rs = slice(r0, r0 + row_block)
            xs = jnp.concatenate([slab_ref[j, pad + r0:pad + r0 + row_block, :]
                                  for j in range(g * gw // V7X_LANES, (g + 1) * gw // V7X_LANES)], axis=1)
            yg = (y_ref[rs, gs] + dskip_ref[:, gs] * xs) * _silu(z_ref[rs, gs])
            yg = yg * lax.rsqrt(jnp.mean(yg * yg, axis=-1, keepdims=True) + EPS) * ng_ref[:, gs]
            yn_ref[rs, gs] = yg.astype(BF16)
        out = out + _dot(yn_ref[:, gs], wout_ref[gs, :])
    o_ref[...] = out

    @pl.when(t == pl.num_programs(1) - 1)
    def _():
        hfin_ref[...] = ht_ref[...].T


def _ssd_prompt(x, norm_layer, layer, g, wz, wxbc, wdt, cw, cb, dtb, alog, dskip, ng, wout, dims):
    b, l, d = x.shape
    n_groups, n_heads, head_dim, d_state = dims
    d_inner = n_heads * head_dim
    cd = wxbc.shape[2]
    ck = cw.shape[1]
    tl = SSD_TILE if l % SSD_TILE == 0 else l
    q = SSD_CHUNK if tl % SSD_CHUNK == 0 else tl
    row = pl.BlockSpec((None, tl, d), lambda i, j: (i, j, 0))
    consts = (wz, wxbc, wdt, cw, cb, dtb, alog, dskip, ng, wout)
    return pl.pallas_call(
        functools.partial(_ssd_prompt_kernel, n_groups=n_groups, n_heads=n_heads, head_dim=head_dim,
                          d_state=d_state, chunk=q, col_block=min(cd, 2 * V7X_LANES), row_block=q),
        out_shape=(jax.ShapeDtypeStruct((b, l, d), F32),
                   jax.ShapeDtypeStruct((b, ck - 1, cd), F32),
                   jax.ShapeDtypeStruct((b, d_inner, d_state), F32)),
        grid=(b, l // tl),
        in_specs=[row, _layer_spec(g, norm_layer)] + [_layer_spec(c, layer) for c in consts],
        out_specs=(row,
                   pl.BlockSpec((None, ck - 1, cd), lambda i, j: (i, 0, 0)),
                   pl.BlockSpec((None, d_inner, d_state), lambda i, j: (i, 0, 0))),
        scratch_shapes=[pltpu.VMEM((cd // V7X_LANES, tl + V7X_SUBLANES, V7X_LANES), F32),
                        pltpu.VMEM((tl, d), BF16),
                        pltpu.VMEM((tl, d_inner), F32),
                        pltpu.VMEM((tl, V7X_LANES), F32),
                        pltpu.VMEM((tl, d_inner), F32),
                        pltpu.VMEM((tl, d_inner), BF16),
                        pltpu.VMEM((d_state, d_inner), F32)],
        compiler_params=_cparams("parallel", "arbitrary"),
        name="ssd_prompt",
    )(x, g, *consts)


def _expand_heads(v, e_ref):
    hi = v.astype(BF16)
    r1 = v - hi.astype(F32)
    mid = r1.astype(BF16)
    lo = (r1 - mid.astype(F32)).astype(BF16)
    e = e_ref[...]
    return _dot(hi, e) + _dot(mid, e) + _dot(lo, e)


def _ssd_sample_in_kernel(x_ref, buf_ref, g_ref, wz_ref, wxbc_ref, wdt_ref, cw_ref, cb_ref, dtb_ref, alog_ref,
                          e_ref, *rest, d_inner, gn, layer):
    z_ref, xs_ref, bm_ref, cmt_ref, xdtt_ref, expa_ref, nbuf_ref = rest[-7:]
    if len(nbuf_ref.shape) == 3:
        for other in range(nbuf_ref.shape[0]):
            if other != layer:
                nbuf_ref[other] = jnp.zeros(nbuf_ref.shape[1:], F32)
        nbuf_ref = nbuf_ref.at[layer]
    x = x_ref[...]
    hn = _rmsnorm(x, g_ref[...]).astype(BF16)
    z_ref[...] = _dot(hn, wz_ref[...])
    new = _dot(hn, wxbc_ref[...])
    cd = new.shape[1]
    ck = cw_ref.shape[0]
    xbc = cb_ref[...] + cw_ref[ck - 1:ck, :] * new
    for k in range(ck - 1):
        xbc = xbc + cw_ref[k:k + 1, :] * buf_ref[:, k * cd:(k + 1) * cd]
    nbuf_ref[:, 0:(ck - 2) * cd] = buf_ref[:, cd:(ck - 1) * cd]
    nbuf_ref[:, (ck - 2) * cd:(ck - 1) * cd] = new
    xbc = _silu(xbc)
    xs = xbc[:, :d_inner]
    dt = _softplus(_dot(hn, wdt_ref[...]) + dtb_ref[...])
    xs_ref[...] = xs
    bm_ref[...] = xbc[:, d_inner:d_inner + gn]
    cmt_ref[...] = xbc[:, d_inner + gn:].T
    xdtt_ref[...] = (xs * _expand_heads(dt, e_ref)).T
    expa_ref[...] = jnp.exp(dt * (-jnp.exp(alog_ref[...])))


def _ssd_sample_state_kernel(h0_ref, bm_ref, cmt_ref, xdtt_ref, expa_ref, *rest,
                             n_groups, n_heads, head_dim, d_state, layer):
    hout_ref, yt_ref = rest[-2:]
    t = pl.program_id(0)
    tb = h0_ref.shape[0]
    ntok = bm_ref.shape[0]
    hpg = n_heads // n_groups
    gw = hpg * head_dim
    if len(hout_ref.shape) == 4:
        for other in range(hout_ref.shape[0]):
            if other != layer:
                hout_ref[other] = jnp.zeros(hout_ref.shape[1:], F32)
        hout_ref = hout_ref.at[layer]

    @pl.when(t == 0)
    def _():
        yt_ref[...] = jnp.zeros(yt_ref.shape, F32)

    tok_rows = lax.broadcasted_iota(jnp.int32, (ntok, d_state), 0)
    tok_cols = lax.broadcasted_iota(jnp.int32, (d_state, ntok), 1)

    def body(i, carry):
        tok = t * tb + i
        ea = expa_ref[pl.ds(tok, 1), :]
        for g in range(n_groups):
            rs = slice(g * gw, (g + 1) * gw)
            ns = slice(g * d_state, (g + 1) * d_state)
            rb = jnp.where(tok_rows == tok, bm_ref[:, ns], 0.0).astype(BF16)
            upd = _dot(xdtt_ref[rs, :].astype(BF16), rb)
            parts = []
            for r in range(hpg):
                h = g * hpg + r
                hr = slice(h * head_dim, (h + 1) * head_dim)
                parts.append(h0_ref[i, hr, :] * ea[:, h:h + 1])
            hnew = jnp.concatenate(parts, axis=0) + upd
            hout_ref[i, rs, :] = hnew
            rc = jnp.where(tok_cols == tok, cmt_ref[ns, :], 0.0).astype(BF16)
            yt_ref[rs, :] = yt_ref[rs, :] + _dot(hnew.astype(BF16), rc)
        return carry

    lax.fori_loop(0, tb, body, 0)


def _ssd_sample_out_kernel(x_ref, yt_ref, xs_ref, z_ref, dskip_ref, ng_ref, wout_ref, o_ref, *, n_groups):
    y = yt_ref[...].T + dskip_ref[...] * xs_ref[...]
    yn = _group_rmsnorm_gate(y, z_ref[...], ng_ref[...], n_groups).astype(BF16)
    o_ref[...] = x_ref[...] + _dot(yn, wout_ref[...])


def _ssd_sample(x, buf_all, nstack, h0_all, hstack, norm_layer, layer, g, wz, wxbc, wdt, cw, cb, dtb, alog, dskip,
                ng, wout, e, dims):
    m, d = x.shape
    n_groups, n_heads, head_dim, d_state = dims
    d_inner = n_heads * head_dim
    gn = n_groups * d_state
    consts = (wz, wxbc, wdt, cw, cb, dtb, alog)
    full = lambda a: pl.BlockSpec(a.shape, lambda *_: (0,) * a.ndim)
    outs = (jax.ShapeDtypeStruct((m, d_inner), F32),
            jax.ShapeDtypeStruct((m, d_inner), F32),
            jax.ShapeDtypeStruct((m, gn), F32),
            jax.ShapeDtypeStruct((gn, m), F32),
            jax.ShapeDtypeStruct((d_inner, m), F32),
            jax.ShapeDtypeStruct((m, V7X_LANES), F32),
            jax.ShapeDtypeStruct(buf_all.shape, F32))
    first = hstack is None
    ins = [x, buf_all, g, *consts, e]
    in_specs = ([full(x), _layer_spec(buf_all, layer), _layer_spec(g, norm_layer)]
                + [_layer_spec(c, layer) for c in consts] + [_const_spec(e)])
    out_specs = [pl.BlockSpec(o.shape, lambda *_, n=len(o.shape): (0,) * n) for o in outs]
    aliases = {}
    if not first:
        out_specs[-1] = pl.BlockSpec((None,) + buf_all.shape[1:], lambda *_: (layer, 0, 0))
        aliases = {len(ins): len(outs) - 1}
        ins.append(nstack)
        in_specs.append(pl.BlockSpec(memory_space=pl.ANY))
    z, xs, bm, cmt, xdtt, expa, nstack = pl.pallas_call(
        functools.partial(_ssd_sample_in_kernel, d_inner=d_inner, gn=gn, layer=layer),
        out_shape=outs,
        grid=(1,),
        in_specs=in_specs,
        out_specs=tuple(out_specs),
        input_output_aliases=aliases,
        compiler_params=_cparams("arbitrary"),
        name="ssd_sample_in",
    )(*ins)

    tile = SAMPLE_STATE_TILE // (h0_all.shape[0] if first else 1)
    tb = tile if tile and m % tile == 0 else m
    st = pl.BlockSpec((None, tb, d_inner, d_state), lambda i: (layer, i, 0, 0))
    ins = [h0_all, bm, cmt, xdtt, expa]
    in_specs = [st, _const_spec(bm), _const_spec(cmt), _const_spec(xdtt), _const_spec(expa)]
    aliases = {}
    if first:
        st_out = pl.BlockSpec((h0_all.shape[0], tb, d_inner, d_state), lambda i: (0, i, 0, 0))
    else:
        st_out = st
        aliases = {len(ins): 0}
        ins.append(hstack)
        in_specs.append(pl.BlockSpec(memory_space=pl.ANY))
    hstack, yt = pl.pallas_call(
        functools.partial(_ssd_sample_state_kernel, n_groups=n_groups, n_heads=n_heads, head_dim=head_dim,
                          d_state=d_state, layer=layer),
        out_shape=(jax.ShapeDtypeStruct(h0_all.shape, F32), jax.ShapeDtypeStruct((d_inner, m), F32)),
        grid=(m // tb,),
        in_specs=in_specs,
        out_specs=(st_out, pl.BlockSpec((d_inner, m), lambda i: (0, 0))),
        input_output_aliases=aliases,
        compiler_params=_cparams("arbitrary"),
        name="ssd_sample_state",
    )(*ins)

    out = pl.pallas_call(
        functools.partial(_ssd_sample_out_kernel, n_groups=n_groups),
        out_shape=jax.ShapeDtypeStruct((m, d), F32),
        grid=(1,),
        in_specs=[full(x), full(yt), full(xs), full(z), _layer_spec(dskip, layer), _layer_spec(ng, layer),
                  _layer_spec(wout, layer)],
        out_specs=full(x),
        compiler_params=_cparams("arbitrary"),
        name="ssd_sample_out",
    )(x, yt, xs, z, dskip, ng, wout)
    return out, nstack, hstack


def _rows(v):
    return v.reshape(v.shape[0], 1, v.shape[1]).astype(F32)


def _pad_lanes(v):
    return jnp.pad(v, [(0, 0)] * (v.ndim - 1) + [(0, V7X_LANES - v.shape[-1])])


def kernel(x_prompt, x_sample, state_conv, state_ssd_conv, state_ssd, p_prompt, p_sample, norm_ffn1, w_ffn1_gate, w_ffn1_up, w_ffn1_down, norm_mix, norm_ffn2, w_ffn2_gate, w_ffn2_up, w_ffn2_down, norm_ple, w_ple_gate, w_ple_proj, cm_w_in, cm_b_in, cm_dw, cm_dw_b, cm_ln_g, cm_ln_b, cm_w_out, cm_b_out, ssd_w_in, ssd_conv_w, ssd_conv_b, ssd_dt_bias, ssd_A_log, ssd_D, ssd_norm, ssd_w_out, final_norm):
    depth = norm_ffn1.shape[0]
    bp, lp, d = x_prompt.shape
    bs = x_sample.shape[0]
    n_heads = ssd_dt_bias.shape[1]
    head_dim, d_state = state_ssd.shape[3], state_ssd.shape[4]
    d_inner = n_heads * head_dim
    conv_dim = ssd_conv_w.shape[2]
    n_groups = (conv_dim - d_inner) // (2 * d_state)
    dims = (n_groups, n_heads, head_dim, d_state)
    assert n_heads <= V7X_LANES and lp % SSD_CHUNK == 0 and x_sample.shape[1] == 1

    bf = lambda w: w.astype(BF16)
    ffn1 = (_rows(norm_ffn1), bf(w_ffn1_gate), bf(w_ffn1_up), bf(w_ffn1_down))
    tail = (_rows(norm_ffn2), bf(w_ffn2_gate), bf(w_ffn2_up), bf(w_ffn2_down), _rows(norm_ple), bf(w_ple_gate),
            bf(w_ple_proj))
    fnorm = final_norm.reshape(1, d)
    nmix = _rows(norm_mix)
    cmw = (bf(cm_w_in), _rows(cm_b_in), cm_dw, _rows(cm_dw_b), _rows(cm_ln_g), _rows(cm_ln_b), bf(cm_w_out),
           _rows(cm_b_out))
    ssw = (bf(ssd_w_in[:, :, :d_inner]), bf(ssd_w_in[:, :, d_inner:d_inner + conv_dim]),
           bf(_pad_lanes(ssd_w_in[:, :, d_inner + conv_dim:])), ssd_conv_w, _rows(ssd_conv_b),
           _rows(_pad_lanes(ssd_dt_bias)), _rows(_pad_lanes(ssd_A_log)),
           _rows(jnp.repeat(ssd_D, head_dim, axis=1)), _rows(ssd_norm), bf(ssd_w_out))
    expand = (lax.broadcasted_iota(jnp.int32, (V7X_LANES, d_inner), 0)
              == lax.broadcasted_iota(jnp.int32, (V7X_LANES, d_inner), 1) // head_dim).astype(BF16)

    xp = x_prompt.reshape(bp * lp, d)
    xsm = x_sample.reshape(bs, d)
    pp = p_prompt.reshape(depth, bp * lp, -1)
    ps = p_sample.reshape(depth, bs, -1)
    h0_all = state_ssd.reshape(state_ssd.shape[0], bs, d_inner, d_state)
    hist_all = state_ssd_conv.reshape(state_ssd_conv.shape[0], bs, -1)
    conv_p, xbc_p, ssm_p = [], [], []
    conv_s = xbc_s = ssm_s = None

    for i in range(depth):
        j = i // 2
        xp = _ffn(xp, i, *ffn1)
        xsm = _ffn(xsm, i, *ffn1)
        if i % 2 == 0:
            xp3, nb = _conv_prompt(xp.reshape(bp, lp, d), i, j, nmix, *cmw)
            xp = xp3.reshape(bp * lp, d)
            conv_p.append(nb)
            xsm, conv_s = _conv_sample(xsm, state_conv, conv_s, i, j, nmix, *cmw)
        else:
            xp3, nb, hf = _ssd_prompt(xp.reshape(bp, lp, d), i, j, nmix, *ssw, dims)
            xp = xp3.reshape(bp * lp, d)
            xbc_p.append(nb)
            ssm_p.append(hf.reshape(bp, n_heads, head_dim, d_state))
            xsm, xbc_s, ssm_s = _ssd_sample(xsm, hist_all, xbc_s, h0_all, ssm_s, i, j, nmix, *ssw, expand, dims)
        last = i == depth - 1
        xp = _ffn_ple(xp, pp, i, *tail, fnorm, final_norm=last)
        xsm = _ffn_ple(xsm, ps, i, *tail, fnorm, final_norm=last)

    return (xp.reshape(bp, lp, d), xsm.reshape(bs, 1, d),
            jnp.stack(conv_p), jnp.stack(xbc_p), jnp.stack(ssm_p),
            conv_s, xbc_s.reshape(state_ssd_conv.shape), ssm_s.reshape(state_ssd.shape))
```

```python
import functools

import jax
import jax.numpy as jnp
from jax import lax
from jax.experimental import pallas as pl
from jax.experimental.pallas import tpu as pltpu

F32 = jnp.float32
BF16 = jnp.bfloat16
EPS = 1e-6

V7X_LANES = 128
V7X_SUBLANES = 8
V7X_MXU_COLS = 256
V7X_VMEM_LIMIT_BYTES = 56 * 1024 * 1024

SSD_CHUNK = 128
SSD_TILE = 512
TOKEN_TILE = 1024
CONV_TILE = 1024
CONV_ROWS = 128
FFN_SPLIT = 2
SAMPLE_CONV_TILE = 32
SAMPLE_STATE_TILE = 8


def _cparams(*sem):
    return pltpu.CompilerParams(dimension_semantics=sem, vmem_limit_bytes=V7X_VMEM_LIMIT_BYTES)


def _const_spec(arr):
    nd = arr.ndim
    return pl.BlockSpec(arr.shape, lambda *_: (0,) * nd, pipeline_mode=pl.Buffered(1))


def _layer_spec(arr, layer):
    nd = arr.ndim
    return pl.BlockSpec((None,) + arr.shape[1:], lambda *_: (layer,) + (0,) * (nd - 1),
                        pipeline_mode=pl.Buffered(1))


def _dot(a, b):
    return jnp.dot(a, b, preferred_element_type=F32)


def _rmsnorm(x, g):
    return x * lax.rsqrt(jnp.mean(x * x, axis=-1, keepdims=True) + EPS) * g


def _layernorm(x, g, b):
    mu = jnp.mean(x, axis=-1, keepdims=True)
    xc = x - mu
    var = jnp.mean(xc * xc, axis=-1, keepdims=True)
    return xc * lax.rsqrt(var + EPS) * g + b


def _sigmoid(x):
    return 0.5 * jnp.tanh(0.5 * x) + 0.5


def _silu(x):
    h = 0.5 * x
    return h * jnp.tanh(h) + h


def _softplus(x):
    return jnp.maximum(x, 0.0) + jnp.log(1.0 + jnp.exp(-jnp.abs(x)))


def _swiglu(x, g, wg_ref, wu_ref, wd_ref):
    h = _rmsnorm(x, g).astype(BF16)
    d_ff = wg_ref.shape[1]
    ncol = -(-d_ff // V7X_MXU_COLS)
    cuts = [min(d_ff, V7X_MXU_COLS * (ncol * c // FFN_SPLIT)) for c in range(FFN_SPLIT)] + [d_ff]
    out = None
    for c in range(FFN_SPLIT):
        cs = slice(cuts[c], cuts[c + 1])
        gate = _dot(h, wg_ref[:, cs])
        up = _dot(h, wu_ref[:, cs])
        act = (_silu(gate) * up).astype(BF16)
        y = _dot(act, wd_ref[cs, :])
        out = y if c == 0 else out + y
    return out


def _ffn_kernel(x_ref, g_ref, wg_ref, wu_ref, wd_ref, o_ref):
    x = x_ref[...]
    o_ref[...] = x + 0.5 * _swiglu(x, g_ref[...], wg_ref, wu_ref, wd_ref)


def _ffn_ple_kernel(x_ref, p_ref, g_ref, wg_ref, wu_ref, wd_ref, gp_ref, wpg_ref, wpp_ref, fn_ref, o_ref,
                    *, final_norm):
    x = x_ref[...]
    x = x + 0.5 * _swiglu(x, g_ref[...], wg_ref, wu_ref, wd_ref)
    h = _rmsnorm(x, gp_ref[...]).astype(BF16)
    gate = _sigmoid(_dot(h, wpg_ref[...]))
    x = x + gate * _dot(p_ref[...].astype(BF16), wpp_ref[...])
    if final_norm:
        x = _rmsnorm(x, fn_ref[...])
    o_ref[...] = x


def _token_tile(m):
    return TOKEN_TILE if m % TOKEN_TILE == 0 else m


def _ffn(x, layer, g, wg, wu, wd):
    m, d = x.shape
    tm = _token_tile(m)
    row = pl.BlockSpec((tm, d), lambda i: (i, 0))
    consts = (g, wg, wu, wd)
    return pl.pallas_call(
        _ffn_kernel,
        out_shape=jax.ShapeDtypeStruct((m, d), F32),
        grid=(m // tm,),
        in_specs=[row] + [_layer_spec(c, layer) for c in consts],
        out_specs=row,
        compiler_params=_cparams("parallel"),
        name="ffn",
    )(x, *consts)


def _ffn_ple(x, p, layer, g, wg, wu, wd, gp, wpg, wpp, fn, final_norm):
    m, d = x.shape
    tm = _token_tile(m)
    row = pl.BlockSpec((tm, d), lambda i: (i, 0))
    prow = pl.BlockSpec((None, tm, p.shape[2]), lambda i: (layer, i, 0))
    consts = (g, wg, wu, wd, gp, wpg, wpp)
    return pl.pallas_call(
        functools.partial(_ffn_ple_kernel, final_norm=final_norm),
        out_shape=jax.ShapeDtypeStruct((m, d), F32),
        grid=(m // tm,),
        in_specs=[row, prow] + [_layer_spec(c, layer) for c in consts] + [_const_spec(fn)],
        out_specs=row,
        compiler_params=_cparams("parallel"),
        name="ffn_ple",
    )(x, p, *consts, fn)


def _glu_in(x, g, win_ref, bin_ref):
    d = x.shape[-1]
    u = _dot(_rmsnorm(x, g).astype(BF16), win_ref[...]) + bin_ref[...]
    return u[:, :d] * _sigmoid(u[:, d:])


def _conv_tail(x, v, lng, lnb, wout_ref, bout_ref):
    v = _silu(_layernorm(v, lng, lnb)).astype(BF16)
    return x + _dot(v, wout_ref[...]) + bout_ref[...]


def _conv_prompt_kernel(x_ref, g_ref, win_ref, bin_ref, dw_ref, dwb_ref, lng_ref, lnb_ref, wout_ref, bout_ref,
                        o_ref, nbuf_ref, slab_ref, v_ref, *, taps, row_block):
    t = pl.program_id(1)
    tl = x_ref.shape[0]
    nslab = slab_ref.shape[0]
    pad = slab_ref.shape[1] - tl
    hist = taps - 1

    @pl.when(t == 0)
    def _():
        slab_ref[:, 0:pad, :] = jnp.zeros((nslab, pad, V7X_LANES), F32)

    x = x_ref[...]
    glu = _glu_in(x, g_ref[...], win_ref, bin_ref)
    for j in range(nslab):
        slab_ref[j, pad:pad + tl, :] = glu[:, j * V7X_LANES:(j + 1) * V7X_LANES]

    for j in range(nslab):
        ls = slice(j * V7X_LANES, (j + 1) * V7X_LANES)
        for r0 in range(0, tl, row_block):
            acc = jnp.broadcast_to(dwb_ref[:, ls], (row_block, V7X_LANES))
            for k in range(taps):
                lo = r0 + pad - hist + k
                acc = acc + dw_ref[k:k + 1, ls] * slab_ref[j, lo:lo + row_block, :]
            v_ref[r0:r0 + row_block, ls] = acc
    o_ref[...] = _conv_tail(x, v_ref[...], lng_ref[...], lnb_ref[...], wout_ref, bout_ref)

    @pl.when(t == pl.num_programs(1) - 1)
    def _():
        for j in range(nslab):
            nbuf_ref[:, j * V7X_LANES:(j + 1) * V7X_LANES] = slab_ref[j, pad + tl - hist:pad + tl, :]

    slab_ref[:, 0:pad, :] = slab_ref[:, tl:tl + pad, :]


def _conv_prompt(x, norm_layer, layer, g, win, b_in, dw, dwb, lng, lnb, wout, bout):
    b, l, d = x.shape
    taps = dw.shape[1]
    tl = CONV_TILE if l % CONV_TILE == 0 else l
    pad = -(-(taps - 1) // V7X_SUBLANES) * V7X_SUBLANES
    row = pl.BlockSpec((None, tl, d), lambda i, j: (i, j, 0))
    consts = (win, b_in, dw, dwb, lng, lnb, wout, bout)
    return pl.pallas_call(
        functools.partial(_conv_prompt_kernel, taps=taps, row_block=min(tl, CONV_ROWS)),
        out_shape=(jax.ShapeDtypeStruct((b, l, d), F32), jax.ShapeDtypeStruct((b, taps - 1, d), F32)),
        grid=(b, l // tl),
        in_specs=[row, _layer_spec(g, norm_layer)] + [_layer_spec(c, layer) for c in consts],
        out_specs=(row, pl.BlockSpec((None, taps - 1, d), lambda i, j: (i, 0, 0))),
        scratch_shapes=[pltpu.VMEM((d // V7X_LANES, tl + pad, V7X_LANES), F32), pltpu.VMEM((tl, d), F32)],
        compiler_params=_cparams("parallel", "arbitrary"),
        name="conv_prompt",
    )(x, g, *consts)


def _conv_sample_kernel(x_ref, buf_ref, g_ref, win_ref, bin_ref, dw_ref, dwb_ref, lng_ref, lnb_ref, wout_ref,
                        bout_ref, *rest, taps, layer):
    o_ref, nbuf_ref = rest[-2:]
    if len(nbuf_ref.shape) == 4:
        for other in range(nbuf_ref.shape[0]):
            if other != layer:
                nbuf_ref[other] = jnp.zeros(nbuf_ref.shape[1:], F32)
        nbuf_ref = nbuf_ref.at[layer]
    x = x_ref[...]
    hist = taps - 1
    glu = _glu_in(x, g_ref[...], win_ref, bin_ref)
    acc = dwb_ref[...] + dw_ref[hist:taps, :] * glu
    for k in range(hist):
        acc = acc + dw_ref[k:k + 1, :] * buf_ref[:, k, :]
    for k in range(hist - 1):
        nbuf_ref[:, k, :] = buf_ref[:, k + 1, :]
    nbuf_ref[:, hist - 1, :] = glu
    o_ref[...] = _conv_tail(x, acc, lng_ref[...], lnb_ref[...], wout_ref, bout_ref)


def _conv_sample(x, buf_all, stack, norm_layer, layer, g, win, b_in, dw, dwb, lng, lnb, wout, bout):
    m, d = x.shape
    taps = dw.shape[1]
    n_layers, _, hist, _ = buf_all.shape
    first = stack is None
    tile = SAMPLE_CONV_TILE // (n_layers if first else 1)
    tb = tile if tile and m % tile == 0 else m
    row = pl.BlockSpec((tb, d), lambda i: (i, 0))
    brow = pl.BlockSpec((None, tb, hist, d), lambda i: (layer, i, 0, 0))
    consts = (win, b_in, dw, dwb, lng, lnb, wout, bout)
    ins = [x, buf_all, g, *consts]
    in_specs = [row, brow, _layer_spec(g, norm_layer)] + [_layer_spec(c, layer) for c in consts]
    aliases = {}
    if first:
        srow = pl.BlockSpec((n_layers, tb, hist, d), lambda i: (0, i, 0, 0))
    else:
        srow = brow
        aliases = {len(ins): 1}
        ins.append(stack)
        in_specs.append(pl.BlockSpec(memory_space=pl.ANY))
    return pl.pallas_call(
        functools.partial(_conv_sample_kernel, taps=taps, layer=layer),
        out_shape=(jax.ShapeDtypeStruct((m, d), F32), jax.ShapeDtypeStruct(buf_all.shape, F32)),
        grid=(m // tb,),
        in_specs=in_specs,
        out_specs=(row, srow),
        input_output_aliases=aliases,
        compiler_params=_cparams("arbitrary"),
        name="conv_sample",
    )(*ins)


def _group_rmsnorm_gate(y, z, ng, n_groups):
    y = y * _silu(z)
    gw = y.shape[-1] // n_groups
    outs = []
    for g in range(n_groups):
        s = y[:, g * gw:(g + 1) * gw]
        outs.append(s * lax.rsqrt(jnp.mean(s * s, axis=-1, keepdims=True) + EPS))
    return jnp.concatenate(outs, axis=-1) * ng


def _prefix_sum_rows(a):
    n = a.shape[0]
    rows = lax.broadcasted_iota(jnp.int32, a.shape, 0)
    sh = 1
    while sh < n:
        a = a + jnp.where(rows >= sh, pltpu.roll(a, sh, 0), 0.0)
        sh *= 2
    return a


def _ssd_prompt_kernel(x_ref, g_ref, wz_ref, wxbc_ref, wdt_ref, cw_ref, cb_ref, dtb_ref, alog_ref, dskip_ref,
                       ng_ref, wout_ref, o_ref, nbuf_ref, hfin_ref, slab_ref, hn_ref, z_ref, dt_ref, y_ref, yn_ref,
                       ht_ref, *, n_groups, n_heads, head_dim, d_state, chunk, col_block, row_block):
    t = pl.program_id(1)
    tl = x_ref.shape[0]
    nslab = slab_ref.shape[0]
    pad = slab_ref.shape[1] - tl
    d_inner = n_heads * head_dim
    hpg = n_heads // n_groups
    gw = hpg * head_dim
    ck = cw_ref.shape[0]
    xslabs = d_inner // V7X_LANES
    assert d_state == V7X_LANES and gw % V7X_LANES == 0 and chunk == d_state

    @pl.when(t == 0)
    def _():
        slab_ref[:, 0:pad, :] = jnp.zeros((nslab, pad, V7X_LANES), F32)
        ht_ref[...] = jnp.zeros(ht_ref.shape, F32)

    for r0 in range(0, tl, row_block):
        hn_ref[r0:r0 + row_block, :] = _rmsnorm(x_ref[r0:r0 + row_block, :], g_ref[...]).astype(BF16)
    hn = hn_ref[...]
    dt_ref[...] = _softplus(_dot(hn, wdt_ref[...]) + dtb_ref[...])
    for c0 in range(0, nslab * V7X_LANES, col_block):
        pre = _dot(hn, wxbc_ref[:, c0:c0 + col_block])
        for jj in range(col_block // V7X_LANES):
            slab_ref[c0 // V7X_LANES + jj, pad:pad + tl, :] = pre[:, jj * V7X_LANES:(jj + 1) * V7X_LANES]
    z_ref[...] = _dot(hn, wz_ref[...])
    for j in range(nslab):
        ls = slice(j * V7X_LANES, (j + 1) * V7X_LANES)
        tail = slab_ref[j, tl:tl + pad, :]
        half_w = [0.5 * cw_ref[k:k + 1, ls] for k in range(ck)]
        half_b = 0.5 * cb_ref[:, ls]
        for r0 in reversed(range(0, tl, row_block)):
            h = jnp.broadcast_to(half_b, (row_block, V7X_LANES))
            for k in range(ck):
                lo = r0 + pad - (ck - 1) + k
                h = h + half_w[k] * slab_ref[j, lo:lo + row_block, :]
            slab_ref[j, pad + r0:pad + r0 + row_block, :] = h * jnp.tanh(h) + h
        slab_ref[j, 0:pad, :] = tail

    @pl.when(t == pl.num_programs(1) - 1)
    def _():
        for j in range(nslab):
            nbuf_ref[:, j * V7X_LANES:(j + 1) * V7X_LANES] = slab_ref[j, pad - (ck - 1):pad, :]

    neg_a = jnp.exp(alog_ref[...])
    hps = V7X_LANES // head_dim
    lane = lax.broadcasted_iota(jnp.int32, (1, V7X_LANES), 1)
    causal = lax.broadcasted_iota(jnp.int32, (chunk, chunk), 0) >= lax.broadcasted_iota(jnp.int32, (chunk, chunk), 1)

    def chunk_body(c, carry):
        r0 = pl.multiple_of(c * chunk, chunk)
        rows = pl.ds(pl.multiple_of(pad + c * chunk, V7X_SUBLANES), chunk)
        dt = dt_ref[pl.ds(r0, chunk), :]
        acum = _prefix_sum_rows(-dt * neg_a)
        a_last = acum[chunk - 1:chunk, :]
        acum_t = acum.T
        dt_t = dt.T
        decdt_t = (dt * jnp.exp(a_last - acum)).T
        cdec = jnp.exp(a_last)
        for g in range(n_groups):
            bg = slab_ref[xslabs + g, rows, :]
            cg = slab_ref[xslabs + n_groups + g, rows, :]
            cbm = lax.dot_general(cg.astype(BF16), bg.astype(BF16), (((1,), (1,)), ((), ())),
                                  preferred_element_type=F32)
            bg_t = bg.T
            for sl in range(g * gw // V7X_LANES, (g + 1) * gw // V7X_LANES):
                ls = slice(sl * V7X_LANES, (sl + 1) * V7X_LANES)
                xs_sl = slab_ref[sl, rows, :]
                hprev = ht_ref[:, ls]
                y_sl = upd_sl = cdec_sl = None
                for u in range(hps):
                    h = sl * hps + u
                    mine = (lane >= u * head_dim) & (lane < (u + 1) * head_dim)
                    xm = jnp.where(mine, xs_sl, 0.0).astype(BF16)
                    hm = jnp.where(mine, hprev, 0.0).astype(BF16)
                    colb = jnp.broadcast_to(acum[:, h:h + 1], (chunk, chunk))
                    lmat = jnp.exp(jnp.where(causal, colb - acum_t[h:h + 1, :], -jnp.inf))
                    lhs = jnp.concatenate([cbm * lmat * dt_t[h:h + 1, :], cg * jnp.exp(colb)], axis=1)
                    yh = _dot(lhs.astype(BF16), jnp.concatenate([xm, hm], axis=0))
                    uh = _dot((bg_t * decdt_t[h:h + 1, :]).astype(BF16), xm)
                    cd = jnp.where(mine, cdec[:, h:h + 1], 0.0)
                    y_sl = yh if u == 0 else y_sl + yh
                    upd_sl = uh if u == 0 else upd_sl + uh
                    cdec_sl = cd if u == 0 else cdec_sl + cd
                y_ref[pl.ds(r0, chunk), ls] = y_sl
                ht_ref[:, ls] = hprev * cdec_sl + upd_sl
        return carry

    lax.fori_loop(0, tl // chunk, chunk_body, 0, unroll=2)

    out = x_ref[...]
    for g in range(n_groups):
        gs = slice(g * gw, (g + 1) * gw)
        for r0 in range(0, tl, row_block):
            rs = slice(r0, r0 + row_block)
            xs = jnp.concatenate([slab_ref[j, pad + r0:pad + r0 + row_block, :]
                                  for j in range(g * gw // V7X_LANES, (g + 1) * gw // V7X_LANES)], axis=1)
            yg = (y_ref[rs, gs] + dskip_ref[:, gs] * xs) * _silu(z_ref[rs, gs])
            yg = yg * lax.rsqrt(jnp.mean(yg * yg, axis=-1, keepdims=True) + EPS) * ng_ref[:, gs]
            yn_ref[rs, gs] = yg.astype(BF16)
        out = out + _dot(yn_ref[:, gs], wout_ref[gs, :])
    o_ref[...] = out

    @pl.when(t == pl.num_programs(1) - 1)
    def _():
        hfin_ref[...] = ht_ref[...].T


def _ssd_prompt(x, norm_layer, layer, g, wz, wxbc, wdt, cw, cb, dtb, alog, dskip, ng, wout, dims):
    b, l, d = x.shape
    n_groups, n_heads, head_dim, d_state = dims
    d_inner = n_heads * head_dim
    cd = wxbc.shape[2]
    ck = cw.shape[1]
    tl = SSD_TILE if l % SSD_TILE == 0 else l
    q = SSD_CHUNK if tl % SSD_CHUNK == 0 else tl
    row = pl.BlockSpec((None, tl, d), lambda i, j: (i, j, 0))
    consts = (wz, wxbc, wdt, cw, cb, dtb, alog, dskip, ng, wout)
    return pl.pallas_call(
        functools.partial(_ssd_prompt_kernel, n_groups=n_groups, n_heads=n_heads, head_dim=head_dim,
                          d_state=d_state, chunk=q, col_block=min(cd, 2 * V7X_LANES), row_block=q),
        out_shape=(jax.ShapeDtypeStruct((b, l, d), F32),
                   jax.ShapeDtypeStruct((b, ck - 1, cd), F32),
                   jax.ShapeDtypeStruct((b, d_inner, d_state), F32)),
        grid=(b, l // tl),
        in_specs=[row, _layer_spec(g, norm_layer)] + [_layer_spec(c, layer) for c in consts],
        out_specs=(row,
                   pl.BlockSpec((None, ck - 1, cd), lambda i, j: (i, 0, 0)),
                   pl.BlockSpec((None, d_inner, d_state), lambda i, j: (i, 0, 0))),
        scratch_shapes=[pltpu.VMEM((cd // V7X_LANES, tl + V7X_SUBLANES, V7X_LANES), F32),
                        pltpu.VMEM((tl, d), BF16),
                        pltpu.VMEM((tl, d_inner), F32),
                        pltpu.VMEM((tl, V7X_LANES), F32),
                        pltpu.VMEM((tl, d_inner), F32),
                        pltpu.VMEM((tl, d_inner), BF16),
                        pltpu.VMEM((d_state, d_inner), F32)],
        compiler_params=_cparams("parallel", "arbitrary"),
        name="ssd_prompt",
    )(x, g, *consts)


def _expand_heads(v, e_ref):
    hi = v.astype(BF16)
    r1 = v - hi.astype(F32)
    mid = r1.astype(BF16)
    lo = (r1 - mid.astype(F32)).astype(BF16)
    e = e_ref[...]
    return _dot(hi, e) + _dot(mid, e) + _dot(lo, e)


def _ssd_sample_in_kernel(x_ref, buf_ref, g_ref, wz_ref, wxbc_ref, wdt_ref, cw_ref, cb_ref, dtb_ref, alog_ref,
                          e_ref, *rest, d_inner, gn, layer):
    z_ref, xs_ref, bm_ref, cmt_ref, xdtt_ref, expa_ref, nbuf_ref = rest[-7:]
    if len(nbuf_ref.shape) == 3:
        for other in range(nbuf_ref.shape[0]):
            if other != layer:
                nbuf_ref[other] = jnp.zeros(nbuf_ref.shape[1:], F32)
        nbuf_ref = nbuf_ref.at[layer]
    x = x_ref[...]
    hn = _rmsnorm(x, g_ref[...]).astype(BF16)
    z_ref[...] = _dot(hn, wz_ref[...])
    new = _dot(hn, wxbc_ref[...])
    cd = new.shape[1]
    ck = cw_ref.shape[0]
    xbc = cb_ref[...] + cw_ref[ck - 1:ck, :] * new
    for k in range(ck - 1):
        xbc = xbc + cw_ref[k:k + 1, :] * buf_ref[:, k * cd:(k + 1) * cd]
    nbuf_ref[:, 0:(ck - 2) * cd] = buf_ref[:, cd:(ck - 1) * cd]
    nbuf_ref[:, (ck - 2) * cd:(ck - 1) * cd] = new
    xbc = _silu(xbc)
    xs = xbc[:, :d_inner]
    dt = _softplus(_dot(hn, wdt_ref[...]) + dtb_ref[...])
    xs_ref[...] = xs
    bm_ref[...] = xbc[:, d_inner:d_inner + gn]
    cmt_ref[...] = xbc[:, d_inner + gn:].T
    xdtt_ref[...] = (xs * _expand_heads(dt, e_ref)).T
    expa_ref[...] = jnp.exp(dt * (-jnp.exp(alog_ref[...])))


def _ssd_sample_state_kernel(h0_ref, bm_ref, cmt_ref, xdtt_ref, expa_ref, *rest,
                             n_groups, n_heads, head_dim, d_state, layer):
    hout_ref, yt_ref = rest[-2:]
    t = pl.program_id(0)
    tb = h0_ref.shape[0]
    ntok = bm_ref.shape[0]
    hpg = n_heads // n_groups
    gw = hpg * head_dim
    if len(hout_ref.shape) == 4:
        for other in range(hout_ref.shape[0]):
            if other != layer:
                hout_ref[other] = jnp.zeros(hout_ref.shape[1:], F32)
        hout_ref = hout_ref.at[layer]

    @pl.when(t == 0)
    def _():
        yt_ref[...] = jnp.zeros(yt_ref.shape, F32)

    tok_rows = lax.broadcasted_iota(jnp.int32, (ntok, d_state), 0)
    tok_cols = lax.broadcasted_iota(jnp.int32, (d_state, ntok), 1)

    def body(i, carry):
        tok = t * tb + i
        ea = expa_ref[pl.ds(tok, 1), :]
        for g in range(n_groups):
            rs = slice(g * gw, (g + 1) * gw)
            ns = slice(g * d_state, (g + 1) * d_state)
            rb = jnp.where(tok_rows == tok, bm_ref[:, ns], 0.0).astype(BF16)
            upd = _dot(xdtt_ref[rs, :].astype(BF16), rb)
            parts = []
            for r in range(hpg):
                h = g * hpg + r
                hr = slice(h * head_dim, (h + 1) * head_dim)
                parts.append(h0_ref[i, hr, :] * ea[:, h:h + 1])
            hnew = jnp.concatenate(parts, axis=0) + upd
            hout_ref[i, rs, :] = hnew
            rc = jnp.where(tok_cols == tok, cmt_ref[ns, :], 0.0).astype(BF16)
            yt_ref[rs, :] = yt_ref[rs, :] + _dot(hnew.astype(BF16), rc)
        return carry

    lax.fori_loop(0, tb, body, 0)


def _ssd_sample_out_kernel(x_ref, yt_ref, xs_ref, z_ref, dskip_ref, ng_ref, wout_ref, o_ref, *, n_groups):
    y = yt_ref[...].T + dskip_ref[...] * xs_ref[...]
    yn = _group_rmsnorm_gate(y, z_ref[...], ng_ref[...], n_groups).astype(BF16)
    o_ref[...] = x_ref[...] + _dot(yn, wout_ref[...])


def _ssd_sample(x, buf_all, nstack, h0_all, hstack, norm_layer, layer, g, wz, wxbc, wdt, cw, cb, dtb, alog, dskip,
                ng, wout, e, dims):
    m, d = x.shape
    n_groups, n_heads, head_dim, d_state = dims
    d_inner = n_heads * head_dim
    gn = n_groups * d_state
    consts = (wz, wxbc, wdt, cw, cb, dtb, alog)
    full = lambda a: pl.BlockSpec(a.shape, lambda *_: (0,) * a.ndim)
    outs = (jax.ShapeDtypeStruct((m, d_inner), F32),
            jax.ShapeDtypeStruct((m, d_inner), F32),
            jax.ShapeDtypeStruct((m, gn), F32),
            jax.ShapeDtypeStruct((gn, m), F32),
            jax.ShapeDtypeStruct((d_inner, m), F32),
            jax.ShapeDtypeStruct((m, V7X_LANES), F32),
            jax.ShapeDtypeStruct(buf_all.shape, F32))
    first = hstack is None
    ins = [x, buf_all, g, *consts, e]
    in_specs = ([full(x), _layer_spec(buf_all, layer), _layer_spec(g, norm_layer)]
                + [_layer_spec(c, layer) for c in consts] + [_const_spec(e)])
    out_specs = [pl.BlockSpec(o.shape, lambda *_, n=len(o.shape): (0,) * n) for o in outs]
    aliases = {}
    if not first:
        out_specs[-1] = pl.BlockSpec((None,) + buf_all.shape[1:], lambda *_: (layer, 0, 0))
        aliases = {len(ins): len(outs) - 1}
        ins.append(nstack)
        in_specs.append(pl.BlockSpec(memory_space=pl.ANY))
    z, xs, bm, cmt, xdtt, expa, nstack = pl.pallas_call(
        functools.partial(_ssd_sample_in_kernel, d_inner=d_inner, gn=gn, layer=layer),
        out_shape=outs,
        grid=(1,),
        in_specs=in_specs,
        out_specs=tuple(out_specs),
        input_output_aliases=aliases,
        compiler_params=_cparams("arbitrary"),
        name="ssd_sample_in",
    )(*ins)

    tile = SAMPLE_STATE_TILE // (h0_all.shape[0] if first else 1)
    tb = tile if tile and m % tile == 0 else m
    st = pl.BlockSpec((None, tb, d_inner, d_state), lambda i: (layer, i, 0, 0))
    ins = [h0_all, bm, cmt, xdtt, expa]
    in_specs = [st, _const_spec(bm), _const_spec(cmt), _const_spec(xdtt), _const_spec(expa)]
    aliases = {}
    if first:
        st_out = pl.BlockSpec((h0_all.shape[0], tb, d_inner, d_state), lambda i: (0, i, 0, 0))
    else:
        st_out = st
        aliases = {len(ins): 0}
        ins.append(hstack)
        in_specs.append(pl.BlockSpec(memory_space=pl.ANY))
    hstack, yt = pl.pallas_call(
        functools.partial(_ssd_sample_state_kernel, n_groups=n_groups, n_heads=n_heads, head_dim=head_dim,
                          d_state=d_state, layer=layer),
        out_shape=(jax.ShapeDtypeStruct(h0_all.shape, F32), jax.ShapeDtypeStruct((d_inner, m), F32)),
        grid=(m // tb,),
        in_specs=in_specs,
        out_specs=(st_out, pl.BlockSpec((d_inner, m), lambda i: (0, 0))),
        input_output_aliases=aliases,
        compiler_params=_cparams("arbitrary"),
        name="ssd_sample_state",
    )(*ins)

    out = pl.pallas_call(
        functools.partial(_ssd_sample_out_kernel, n_groups=n_groups),
        out_shape=jax.ShapeDtypeStruct((m, d), F32),
        grid=(1,),
        in_specs=[full(x), full(yt), full(xs), full(z), _layer_spec(dskip, layer), _layer_spec(ng, layer),
                  _layer_spec(wout, layer)],
        out_specs=full(x),
        compiler_params=_cparams("arbitrary"),
        name="ssd_sample_out",
    )(x, yt, xs, z, dskip, ng, wout)
    return out, nstack, hstack


def _rows(v):
    return v.reshape(v.shape[0], 1, v.shape[1]).astype(F32)


def _pad_lanes(v):
    return jnp.pad(v, [(0, 0)] * (v.ndim - 1) + [(0, V7X_LANES - v.shape[-1])])


def kernel(x_prompt, x_sample, state_conv, state_ssd_conv, state_ssd, p_prompt, p_sample, norm_ffn1, w_ffn1_gate, w_ffn1_up, w_ffn1_down, norm_mix, norm_ffn2, w_ffn2_gate, w_ffn2_up, w_ffn2_down, norm_ple, w_ple_gate, w_ple_proj, cm_w_in, cm_b_in, cm_dw, cm_dw_b, cm_ln_g, cm_ln_b, cm_w_out, cm_b_out, ssd_w_in, ssd_conv_w, ssd_conv_b, ssd_dt_bias, ssd_A_log, ssd_D, ssd_norm, ssd_w_out, final_norm):
    depth = norm_ffn1.shape[0]
    bp, lp, d = x_prompt.shape
    bs = x_sample.shape[0]
    n_heads = ssd_dt_bias.shape[1]
    head_dim, d_state = state_ssd.shape[3], state_ssd.shape[4]
    d_inner = n_heads * head_dim
    conv_dim = ssd_conv_w.shape[2]
    n_groups = (conv_dim - d_inner) // (2 * d_state)
    dims = (n_groups, n_heads, head_dim, d_state)
    assert n_heads <= V7X_LANES and lp % SSD_CHUNK == 0 and x_sample.shape[1] == 1

    bf = lambda w: w.astype(BF16)
    ffn1 = (_rows(norm_ffn1), bf(w_ffn1_gate), bf(w_ffn1_up), bf(w_ffn1_down))
    tail = (_rows(norm_ffn2), bf(w_ffn2_gate), bf(w_ffn2_up), bf(w_ffn2_down), _rows(norm_ple), bf(w_ple_gate),
            bf(w_ple_proj))
    fnorm = final_norm.reshape(1, d)
    nmix = _rows(norm_mix)
    cmw = (bf(cm_w_in), _rows(cm_b_in), cm_dw, _rows(cm_dw_b), _rows(cm_ln_g), _rows(cm_ln_b), bf(cm_w_out),
           _rows(cm_b_out))
    ssw = (bf(ssd_w_in[:, :, :d_inner]), bf(ssd_w_in[:, :, d_inner:d_inner + conv_dim]),
           bf(_pad_lanes(ssd_w_in[:, :, d_inner + conv_dim:])), ssd_conv_w, _rows(ssd_conv_b),
           _rows(_pad_lanes(ssd_dt_bias)), _rows(_pad_lanes(ssd_A_log)),
           _rows(jnp.repeat(ssd_D, head_dim, axis=1)), _rows(ssd_norm), bf(ssd_w_out))
    expand = (lax.broadcasted_iota(jnp.int32, (V7X_LANES, d_inner), 0)
              == lax.broadcasted_iota(jnp.int32, (V7X_LANES, d_inner), 1) // head_dim).astype(BF16)

    xp = x_prompt.reshape(bp * lp, d)
    xsm = x_sample.reshape(bs, d)
    pp = p_prompt.reshape(depth, bp * lp, -1)
    ps = p_sample.reshape(depth, bs, -1)
    h0_all = state_ssd.reshape(state_ssd.shape[0], bs, d_inner, d_state)
    hist_all = state_ssd_conv.reshape(state_ssd_conv.shape[0], bs, -1)
    conv_p, xbc_p, ssm_p = [], [], []
    conv_s = xbc_s = ssm_s = None

    for i in range(depth):
        j = i // 2
        xp = _ffn(xp, i, *ffn1)
        xsm = _ffn(xsm, i, *ffn1)
        if i % 2 == 0:
            xp3, nb = _conv_prompt(xp.reshape(bp, lp, d), i, j, nmix, *cmw)
            xp = xp3.reshape(bp * lp, d)
            conv_p.append(nb)
            xsm, conv_s = _conv_sample(xsm, state_conv, conv_s, i, j, nmix, *cmw)
        else:
            xp3, nb, hf = _ssd_prompt(xp.reshape(bp, lp, d), i, j, nmix, *ssw, dims)
            xp = xp3.reshape(bp * lp, d)
            xbc_p.append(nb)
            ssm_p.append(hf.reshape(bp, n_heads, head_dim, d_state))
            xsm, xbc_s, ssm_s = _ssd_sample(xsm, hist_all, xbc_s, h0_all, ssm_s, i, j, nmix, *ssw, expand, dims)
        last = i == depth - 1
        xp = _ffn_ple(xp, pp, i, *tail, fnorm, final_norm=last)
        xsm = _ffn_ple(xsm, ps, i, *tail, fnorm, final_norm=last)

    return (xp.reshape(bp, lp, d), xsm.reshape(bs, 1, d),
            jnp.stack(conv_p), jnp.stack(xbc_p), jnp.stack(ssm_p),
            conv_s, xbc_s.reshape(state_ssd_conv.shape), ssm_s.reshape(state_ssd.shape))
```
